```python
import jax, jax.numpy as jnp
from jax import lax
import numpy as np

D_MODEL = 1024
BATCH = 16
SEQ = 256
DEPTH = 2
DEC_BATCH = 4
DEC_SEQ = 2048
PAST_LEN = 512

GRID_W = 64
N_MIXERS = 2
N_RWKV = (DEPTH + 1) // 2
N_SGU = DEPTH // 2
RW_HEAD_DIM = 64
RW_HEADS = D_MODEL // RW_HEAD_DIM
DECAY_LORA = 64
AAA_LORA = 64
GATE_LORA = 128
N_SHIFT = 6
SGU_WIDTH = 2 * D_MODEL
SGU_GROUPS = 8
CHUNK = 128
FFN_HIDDEN = 2816
CONV_W = 3
N_MOD = 6
RMS_EPS = 1e-6
LN_EPS = 1e-5
GN_EPS = 64e-5

kernel_name = "hybrid_rwkv7_sgu_convffn_diffusion_step"


def rmsnorm(x, g):
    xf = x.astype(jnp.float32)
    y = xf * lax.rsqrt(jnp.mean(xf * xf, axis=-1, keepdims=True) + RMS_EPS)
    return (y * g.astype(jnp.float32)).astype(x.dtype)


def layernorm(x, g, b):
    xf = x.astype(jnp.float32)
    mu = jnp.mean(xf, axis=-1, keepdims=True)
    var = jnp.mean(jnp.square(xf - mu), axis=-1, keepdims=True)
    y = (xf - mu) * lax.rsqrt(var + LN_EPS)
    return (y * g.astype(jnp.float32) + b.astype(jnp.float32)).astype(x.dtype)


def shift_seq(x):
    prev = jnp.pad(x[:, :-1], ((0, 0), (1, 0), (0, 0)))
    nxt = jnp.pad(x[:, 1:], ((0, 0), (0, 1), (0, 0)))
    return prev, nxt


def wkv_scan(r, w, k, v, aa, bb, s0, reverse):
    xs = tuple(jnp.moveaxis(t, 1, 0) for t in (r, w, k, v, aa, bb))

    def step(S, inp):
        r_t, w_t, k_t, v_t, a_t, b_t = inp
        sa = jnp.einsum('bhij,bhj->bhi', S, a_t)
        S = S * w_t[:, :, None, :] + sa[..., None] * b_t[:, :, None, :] + v_t[..., None] * k_t[:, :, None, :]
        y = jnp.einsum('bhij,bhj->bhi', S, r_t)
        return S, y

    S, ys = lax.scan(step, s0, xs, reverse=reverse)
    return jnp.moveaxis(ys, 0, 1), S


def rwkv7_mix(h, s0_fwd, s0_bwd, mu, w_r, w_k, w_v, w_o, w0, w1, w2, a0, a1, a2, g1, g2,
              k_k, k_a, r_k, lnx_w, lnx_b):
    B, T, D = h.shape
    H, K = RW_HEADS, RW_HEAD_DIM
    f32 = jnp.float32
    heads = lambda t: t.reshape(B, T, H, K).astype(f32)
    prev, nxt = shift_seq(h)
    xs = h[:, :, None, :] + (prev - h)[:, :, None, :] * mu[0] + (nxt - h)[:, :, None, :] * mu[1]
    xr, xw, xk, xv, xa, xg = [xs[:, :, i] for i in range(N_SHIFT)]
    r = xr @ w_r
    k = xk @ w_k
    v = xv @ w_v
    g = jax.nn.sigmoid(xg @ g1) @ g2
    rh, kh, vh = heads(r), heads(k), heads(v)
    kk = kh * k_k.reshape(H, K).astype(f32)
    kk = kk / jnp.maximum(jnp.sqrt(jnp.sum(kk * kk, axis=-1, keepdims=True)), 1e-12)
    y = jnp.zeros_like(rh)
    k_bonus = jnp.zeros_like(kh)
    finals = []
    for d, (s0, rev) in enumerate(((s0_fwd, False), (s0_bwd, True))):
        w_raw = (w0[d] + jnp.tanh(xw @ w1[d]) @ w2[d]).astype(f32)
        decay = jnp.exp(-jnp.exp(-jax.nn.softplus(-w_raw) - 0.5))
        a = heads(jax.nn.sigmoid((a0[d] + (xa @ a1[d]) @ a2[d]).astype(f32)))
        kd = kh * (1.0 + (a - 1.0) * k_a.reshape(H, K).astype(f32))
        yd, sd = wkv_scan(rh, heads(decay), kd, vh, -kk, kk * a, s0.astype(f32), rev)
        y = y + yd
        k_bonus = k_bonus + kd
        finals.append(sd.astype(h.dtype))
    mean = jnp.mean(y, axis=-1, keepdims=True)
    var = jnp.mean(jnp.square(y - mean), axis=-1, keepdims=True)
    yn = (y - mean) * lax.rsqrt(var + GN_EPS)
    yn = yn * lnx_w.reshape(H, K).astype(f32) + lnx_b.reshape(H, K).astype(f32)
    yn = yn + jnp.sum(rh * k_bonus * r_k.astype(f32), axis=-1, keepdims=True) * vh
    out = (yn.reshape(B, T, D).astype(h.dtype) * g) @ w_o
    return out, finals[0], finals[1]


def sgu_mix(h, w_in, ln_w, ln_b, w_s, b_s, w_out):
    B, T, _ = h.shape
    z = jax.nn.gelu(h @ w_in, approximate=False)
    u, v = jnp.split(z, 2, axis=-1)
    v = layernorm(v, ln_w, ln_b)
    vc = v.reshape(B, T // CHUNK, CHUNK, SGU_GROUPS, SGU_WIDTH // SGU_GROUPS)
    vm = jnp.einsum('gpq,bnqgc->bnpgc', w_s, vc) + b_s.T[:, :, None]
    return (u * vm.reshape(B, T, SGU_WIDTH)) @ w_out


def conv_ffn(h, w_up, w_conv, b_conv, w_down, on_grid):
    B, T, _ = h.shape
    up = h @ w_up
    C = up.shape[-1]
    if on_grid:
        rows = T // GRID_W
        img = up.reshape(B, rows, GRID_W, C)
        img = lax.conv_general_dilated(img, w_conv[:, :, None, :], (1, 1), 'SAME',
                                       dimension_numbers=('NHWC', 'HWIO', 'NHWC'),
                                       feature_group_count=C)
        up = img.reshape(B, T, C)
    else:
        prev, nxt = shift_seq(up)
        up = prev * w_conv[1, 0] + up * w_conv[1, 1] + nxt * w_conv[1, 2]
    up = up + b_conv
    val, gate = jnp.split(up, 2, axis=-1)
    return (jax.nn.silu(gate) * val) @ w_down


def setup_inputs(seed: int = 0) -> dict:
    key = jax.random.key(seed)
    keys = iter(jax.random.split(key, 48))
    D, H, K, F, E, G = D_MODEL, RW_HEADS, RW_HEAD_DIM, FFN_HIDDEN, SGU_WIDTH, SGU_GROUPS

    def nrm(shape, scale):
        return jax.random.normal(next(keys), shape, jnp.float32) * scale

    def gain(shape):
        return 1.0 + nrm(shape, 0.05)

    return {
        "x_prompt": nrm((BATCH, SEQ, D), 1.0),
        "x_sample": nrm((DEC_BATCH, DEC_SEQ, D), 1.0),
        "state_ctx_fwd": nrm((DEC_BATCH, N_RWKV, H, K, K), 0.5),
        "state_ctx_bwd": nrm((DEC_BATCH, N_RWKV, H, K, K), 0.5),
        "c": nrm((DEC_BATCH, D), 1.0),
        "c_ctx": nrm((D,), 1.0),
        "ada_w": nrm((DEPTH, D, N_MOD * D), 0.5 * D ** -0.5),
        "ada_b": nrm((DEPTH, N_MOD * D), 0.02),
        "norm_mix": gain((DEPTH, D)),
        "norm_ffn": gain((DEPTH, D)),
        "ffn_up": nrm((DEPTH, D, 2 * F), D ** -0.5),
        "ffn_conv": nrm((DEPTH, CONV_W, CONV_W, 2 * F), 1.0 / 3.0),
        "ffn_conv_b": nrm((DEPTH, 2 * F), 0.02),
        "ffn_down": nrm((DEPTH, F, D), F ** -0.5),
        "norm_final": gain((D,)),
        "rw_mu": jax.random.uniform(next(keys), (N_RWKV, 2, N_SHIFT, D), jnp.float32, 0.0, 0.5),
        "rw_wr": nrm((N_RWKV, D, D), D ** -0.5),
        "rw_wk": nrm((N_RWKV, D, D), D ** -0.5),
        "rw_wv": nrm((N_RWKV, D, D), D ** -0.5),
        "rw_wo": nrm((N_RWKV, D, D), D ** -0.5),
        "rw_w0": nrm((N_RWKV, 2, D), 0.5),
        "rw_w1": nrm((N_RWKV, 2, D, DECAY_LORA), D ** -0.5),
        "rw_w2": nrm((N_RWKV, 2, DECAY_LORA, D), 0.1 * DECAY_LORA ** -0.5),
        "rw_a0": nrm((N_RWKV, 2, D), 0.5),
        "rw_a1": nrm((N_RWKV, 2, D, AAA_LORA), D ** -0.5),
        "rw_a2": nrm((N_RWKV, 2, AAA_LORA, D), 0.1 * AAA_LORA ** -0.5),
        "rw_g1": nrm((N_RWKV, D, GATE_LORA), D ** -0.5),
        "rw_g2": nrm((N_RWKV, GATE_LORA, D), GATE_LORA ** -0.5),
        "rw_kk": 0.85 + nrm((N_RWKV, D), 0.05),
        "rw_ka": gain((N_RWKV, D)),
        "rw_rk": nrm((N_RWKV, H, K), 0.1),
        "rw_lnx_w": gain((N_RWKV, D)),
        "rw_lnx_b": nrm((N_RWKV, D), 0.02),
        "sg_in": nrm((N_SGU, D, 2 * E), D ** -0.5),
        "sg_ln_w": gain((N_SGU, E)),
        "sg_ln_b": nrm((N_SGU, E), 0.02),
        "sg_ws": nrm((N_SGU, G, CHUNK, CHUNK), CHUNK ** -0.5),
        "sg_bs": gain((N_SGU, G, CHUNK)),
        "sg_out": nrm((N_SGU, E, D), E ** -0.5),
    }


def reference(x_prompt, x_sample, state_ctx_fwd, state_ctx_bwd, c, c_ctx,
              ada_w, ada_b, norm_mix, norm_ffn, ffn_up, ffn_conv, ffn_conv_b, ffn_down, norm_final,
              rw_mu, rw_wr, rw_wk, rw_wv, rw_wo, rw_w0, rw_w1, rw_w2, rw_a0, rw_a1, rw_a2,
              rw_g1, rw_g2, rw_kk, rw_ka, rw_rk, rw_lnx_w, rw_lnx_b,
              sg_in, sg_ln_w, sg_ln_b, sg_ws, sg_bs, sg_out):

    def run_stream(x, cond, s0_fwd, s0_bwd, on_grid):
        new_f, new_b = [], []
        sc = jax.nn.silu(cond)
        for i in range(DEPTH):
            mod = (sc @ ada_w[i] + ada_b[i])[:, None, :]
            sh1, sc1, gt1, sh2, sc2, gt2 = jnp.split(mod, N_MOD, axis=-1)
            h = rmsnorm(x, norm_mix[i]) * (1.0 + sc1) + sh1
            j = i // N_MIXERS
            if i % N_MIXERS == 0:
                out, sf, sb = rwkv7_mix(h, s0_fwd[:, j], s0_bwd[:, j], rw_mu[j], rw_wr[j], rw_wk[j], rw_wv[j],
                                        rw_wo[j], rw_w0[j], rw_w1[j], rw_w2[j], rw_a0[j], rw_a1[j], rw_a2[j],
                                        rw_g1[j], rw_g2[j], rw_kk[j], rw_ka[j], rw_rk[j], rw_lnx_w[j], rw_lnx_b[j])
                new_f.append(sf)
                new_b.append(sb)
            else:
                out = sgu_mix(h, sg_in[j], sg_ln_w[j], sg_ln_b[j], sg_ws[j], sg_bs[j], sg_out[j])
            x = x + gt1 * out
            h = rmsnorm(x, norm_ffn[i]) * (1.0 + sc2) + sh2
            x = x + gt2 * conv_ffn(h, ffn_up[i], ffn_conv[i], ffn_conv_b[i], ffn_down[i], on_grid)
        return rmsnorm(x, norm_final), jnp.stack(new_f, axis=1), jnp.stack(new_b, axis=1)

    zero_state = jnp.zeros((x_prompt.shape[0], N_RWKV, RW_HEADS, RW_HEAD_DIM, RW_HEAD_DIM), x_prompt.dtype)
    y_prompt, new_state_fwd, new_state_bwd = run_stream(x_prompt, c_ctx[None, :], zero_state, zero_state, False)

    y_sample, _, _ = run_stream(x_sample, c, state_ctx_fwd, state_ctx_bwd, True)

    return (y_prompt, y_sample, new_state_fwd, new_state_bwd)
```

```python
import functools
import math

import jax
import jax.numpy as jnp
from jax import lax
from jax.experimental import pallas as pl
from jax.experimental.pallas import tpu as pltpu

RMS_EPS = 1e-6
LN_EPS = 1e-5
GN_EPS = 64e-5
N_MOD = 6
HEAD_DIM = 64
SGU_CHUNK = 128
GRID_W = 64

LANES = 128
SUBLANES = 8
VMEM_LIMIT = 56 * 1024 * 1024

TOKEN_TILE = 256
SCAN_BATCH = 4
SCAN_TCHUNK = 32
POST_TCHUNK = 16

BF16 = jnp.bfloat16
F32 = jnp.float32


def _params(*sem):
    return pltpu.CompilerParams(dimension_semantics=sem, vmem_limit_bytes=VMEM_LIMIT)


def _dot(a, b):
    return jnp.dot(a.astype(BF16), b, preferred_element_type=F32)


def _norm_mod(x, g, shift, scale):
    ms = jnp.mean(x * x, axis=-1, keepdims=True)
    return x * lax.rsqrt(ms + RMS_EPS) * g * (1.0 + scale) + shift


def _sigmoid(x):
    return 1.0 / (1.0 + jnp.exp(-x))


def _mod_kernel(c_ref, w_ref, b_ref, o_ref):
    c = c_ref[...]
    sc = c * _sigmoid(c)
    o_ref[0] = jnp.dot(sc, w_ref[0], preferred_element_type=F32,
                       precision=lax.Precision.HIGHEST) + b_ref[0]


def _modulation(cond8, ada_w, ada_b):
    depth, d, n = ada_w.shape
    tn = n // 4
    return pl.pallas_call(
        _mod_kernel,
        grid=(depth, n // tn),
        in_specs=[
            pl.BlockSpec((SUBLANES, d), lambda l, j: (0, 0)),
            pl.BlockSpec((1, d, tn), lambda l, j: (l, 0, j)),
            pl.BlockSpec((1, 1, tn), lambda l, j: (l, 0, j)),
        ],
        out_specs=pl.BlockSpec((1, SUBLANES, tn), lambda l, j: (l, 0, j)),
        out_shape=jax.ShapeDtypeStruct((depth, SUBLANES, n), F32),
        compiler_params=_params("arbitrary", "arbitrary"),
        name="modulation",
    )(cond8, ada_w, ada_b.reshape(depth, 1, n))


def _mod_spec(mod):
    if mod.shape[0] == 1:
        return pl.BlockSpec((1, 1, mod.shape[2]), lambda b, i: (0, 0, 0))
    return pl.BlockSpec((1, 1, mod.shape[2]), lambda b, i: (b, 0, 0))


def _const_spec(a):
    nd = a.ndim
    return pl.BlockSpec(a.shape, lambda *_: (0,) * nd)


def _rw_proj_kernel(x_ref, xp_ref, xn_ref, mod_ref, nw_ref, mu_ref,
                    wr_ref, wk_ref, wv_ref, w1_ref, a1_ref, g1_ref,
                    w2_ref, a2_ref, w0_ref, a0_ref,
                    r_ref, k_ref, v_ref, df_ref, db_ref, af_ref, ab_ref, g_ref):
    i = pl.program_id(1)
    last = pl.num_programs(1) - 1
    d = x_ref.shape[2]
    tt = x_ref.shape[1]
    nw = nw_ref[...]
    shift = mod_ref[0, :, 0:d]
    scale = mod_ref[0, :, d:2 * d]
    h = _norm_mod(x_ref[0], nw, shift, scale)
    hp = _norm_mod(xp_ref[0], nw, shift, scale)[SUBLANES - 1:SUBLANES, :]
    hn = _norm_mod(xn_ref[0], nw, shift, scale)[0:1, :]
    hp = jnp.where(i == 0, 0.0, hp)
    hn = jnp.where(i == last, 0.0, hn)
    rows = lax.broadcasted_iota(jnp.int32, (tt, 1), 0)
    prev = jnp.where(rows == 0, hp, pltpu.roll(h, 1, axis=0))
    nxt = jnp.where(rows == tt - 1, hn, pltpu.roll(h, tt - 1, axis=0))
    dp = prev - h
    dn = nxt - h

    def shifted(idx):
        return h + dp * mu_ref[0, idx:idx + 1, :] + dn * mu_ref[1, idx:idx + 1, :]

    r_ref[0] = _dot(shifted(0), wr_ref[...])
    k_ref[0] = _dot(shifted(2), wk_ref[...])
    v_ref[0] = _dot(shifted(3), wv_ref[...])
    g_ref[0] = _sigmoid(_dot(shifted(5), g1_ref[...]))
    lw = jnp.tanh(_dot(shifted(1), w1_ref[...]))
    la = _dot(shifted(4), a1_ref[...])
    lora = w2_ref.shape[1]
    decay_scale = math.exp(-0.5)
    for dirn, (d_ref, a_ref) in enumerate(((df_ref, af_ref), (db_ref, ab_ref))):
        w_raw = w0_ref[dirn:dirn + 1, :] + _dot(lw[:, dirn * lora:(dirn + 1) * lora], w2_ref[dirn])
        d_ref[0] = jnp.exp(-decay_scale * _sigmoid(w_raw))
        a_raw = a0_ref[dirn:dirn + 1, :] + _dot(la[:, dirn * lora:(dirn + 1) * lora], a2_ref[dirn])
        a_ref[0] = _sigmoid(a_raw)


def _rw_proj(x, mod, nw, mu, wr, wk, wv, w1, a1, g1, w2, a2, w0, a0):
    b, t, d = x.shape
    tt = TOKEN_TILE
    nt = t // tt
    hb = tt // SUBLANES
    nhb = t // SUBLANES
    tok = pl.BlockSpec((1, tt, d), lambda bi, i: (bi, i, 0))
    prev = pl.BlockSpec((1, SUBLANES, d), lambda bi, i: (bi, jnp.maximum(i * hb - 1, 0), 0))
    nxt = pl.BlockSpec((1, SUBLANES, d), lambda bi, i: (bi, jnp.minimum((i + 1) * hb, nhb - 1), 0))
    consts = (nw, mu, wr, wk, wv, w1, a1, g1, w2, a2, w0, a0)
    gl = g1.shape[1]
    out_shape = [jax.ShapeDtypeStruct((b, t, d), F32)] * 7 + [jax.ShapeDtypeStruct((b, t, gl), F32)]
    out_specs = [tok] * 7 + [pl.BlockSpec((1, tt, gl), lambda bi, i: (bi, i, 0))]
    return pl.pallas_call(
        _rw_proj_kernel,
        grid=(b, nt),
        in_specs=[tok, prev, nxt, _mod_spec(mod)] + [_const_spec(c) for c in consts],
        out_specs=out_specs,
        out_shape=out_shape,
        compiler_params=_params("arbitrary", "arbitrary"),
        name="rw_proj",
    )(x, x, x, mod, *consts)


def _scan_kernel(r_ref, k_ref, v_ref, w_ref, a_ref, s0_ref, kk_ref, ka_ref,
                 y_ref, sfin_ref, s_ref, ops_ref):
    c = pl.program_id(1)
    tc = r_ref.shape[1]
    hd = r_ref.shape[2]

    @pl.when(c == 0)
    def _():
        s_ref[...] = s0_ref[0]

    def step(t, carry):
        kraw = k_ref[0, t]
        a = a_ref[0, t]
        kkv = kraw * kk_ref[...]
        nrm = jnp.sqrt(jnp.sum(kkv * kkv, axis=0, keepdims=True))
        kk = kkv / jnp.maximum(nrm, 1e-12)
        ops_ref[0] = -kk
        ops_ref[1] = kk * a
        ops_ref[2] = kraw * (1.0 + (a - 1.0) * ka_ref[...])
        v = v_ref[0, t]

        acc = [jnp.zeros((hd, LANES), F32), jnp.zeros((hd, LANES), F32)]
        for j in range(hd):
            acc[j % 2] = acc[j % 2] + s_ref[j] * ops_ref[0, j:j + 1, :]
        sa = acc[0] + acc[1]

        acc = [jnp.zeros((hd, LANES), F32), jnp.zeros((hd, LANES), F32)]
        for j in range(hd):
            sn = (s_ref[j] * w_ref[0, t, j:j + 1, :] + sa * ops_ref[1, j:j + 1, :]
                  + v * ops_ref[2, j:j + 1, :])
            s_ref[j] = sn
            acc[j % 2] = acc[j % 2] + sn * r_ref[0, t, j:j + 1, :]
        y_ref[0, t] = acc[0] + acc[1]
        return carry

    lax.fori_loop(0, tc, step, 0)

    @pl.when(c == pl.num_programs(1) - 1)
    def _():
        sfin_ref[0] = s_ref[...]


def _scan(r, k, v, w, a, s0, kk, ka):
    g, t, hd, _ = r.shape
    tc = SCAN_TCHUNK
    step = pl.BlockSpec((1, tc, hd, LANES), lambda gi, c: (gi, c, 0, 0))
    state = pl.BlockSpec((1, hd, hd, LANES), lambda gi, c: (gi, 0, 0, 0))
    return pl.pallas_call(
        _scan_kernel,
        grid=(g, t // tc),
        in_specs=[step] * 5 + [state, _const_spec(kk), _const_spec(ka)],
        out_specs=[step, state],
        out_shape=[jax.ShapeDtypeStruct((g, t, hd, LANES), F32),
                   jax.ShapeDtypeStruct((g, hd, hd, LANES), F32)],
        scratch_shapes=[pltpu.VMEM((hd, hd, LANES), F32), pltpu.VMEM((3, hd, LANES), F32)],
        compiler_params=_params("arbitrary", "arbitrary"),
        name="wkv_scan",
    )(r, k, v, w, a, s0, kk, ka)


def _rw_post_kernel(y_ref, ym_ref, r_ref, k_ref, v_ref, a_ref, am_ref,
                    ka_ref, rk_ref, lw_ref, lb_ref, o_ref):
    tp = y_ref.shape[1]
    half = LANES // 2

    def step(t, carry):
        m = tp - 1 - t
        y = y_ref[0, t] + pltpu.roll(ym_ref[0, m], half, axis=1)
        a1 = a_ref[0, t]
        a2 = pltpu.roll(am_ref[0, m], half, axis=1)
        k = k_ref[0, t]
        ka = ka_ref[...]
        kb = k * (1.0 + (a1 - 1.0) * ka) + k * (1.0 + (a2 - 1.0) * ka)
        mean = jnp.mean(y, axis=0, keepdims=True)
        yc = y - mean
        var = jnp.mean(yc * yc, axis=0, keepdims=True)
        yn = yc * lax.rsqrt(var + GN_EPS) * lw_ref[...] + lb_ref[...]
        bonus = jnp.sum(r_ref[0, t] * kb * rk_ref[...], axis=0, keepdims=True)
        o_ref[0, t] = yn + bonus * v_ref[0, t]
        return carry

    lax.fori_loop(0, tp, step, 0)


def _rw_post(y, r, k, v, a, ka, rk, lw, lb):
    g, t, hd, _ = y.shape
    tp = POST_TCHUNK
    nb = t // tp
    cur = pl.BlockSpec((1, tp, hd, LANES), lambda gi, c: (gi, c, 0, 0))
    mir = pl.BlockSpec((1, tp, hd, LANES), lambda gi, c: (gi, nb - 1 - c, 0, 0))
    consts = (ka, rk, lw, lb)
    return pl.pallas_call(
        _rw_post_kernel,
        grid=(g, nb // 2),
        in_specs=[cur, mir, cur, cur, cur, cur, mir] + [_const_spec(c) for c in consts],
        out_specs=cur,
        out_shape=jax.ShapeDtypeStruct((g, t // 2, hd, LANES), F32),
        compiler_params=_params("arbitrary", "arbitrary"),
        name="rw_post",
    )(y, y, r, k, v, a, a, *consts)


def _rw_out_kernel(x_ref, yn_ref, g_ref, mod_ref, g2_ref, wo_ref, o_ref):
    d = x_ref.shape[2]
    gate = mod_ref[0, :, 2 * d:3 * d]
    g = _dot(g_ref[0], g2_ref[...])
    out = _dot(yn_ref[0] * g, wo_ref[...])
    o_ref[0] = x_ref[0] + gate * out


def _rw_out(x, yn, g, mod, g2, wo):
    b, t, d = x.shape
    tt = TOKEN_TILE
    tok = pl.BlockSpec((1, tt, d), lambda bi, i: (bi, i, 0))
    gspec = pl.BlockSpec((1, tt, g.shape[2]), lambda bi, i: (bi, i, 0))
    return pl.pallas_call(
        _rw_out_kernel,
        grid=(b, t // tt),
        in_specs=[tok, tok, gspec, _mod_spec(mod), _const_spec(g2), _const_spec(wo)],
        out_specs=tok,
        out_shape=jax.ShapeDtypeStruct((b, t, d), F32),
        compiler_params=_params("arbitrary", "arbitrary"),
        name="rw_out",
    )(x, yn, g, mod, g2, wo)


def _ffn_up_kernel(x_ref, mod_ref, nw_ref, w_ref, o_ref):
    d = x_ref.shape[2]
    shift = mod_ref[0, :, 3 * d:4 * d]
    scale = mod_ref[0, :, 4 * d:5 * d]
    h = _norm_mod(x_ref[0], nw_ref[...], shift, scale)
    o_ref[0] = _dot(h, w_ref[...])


def _ffn_up(x, mod, nw, w_up):
    b, t, d = x.shape
    n = w_up.shape[1]
    tt = TOKEN_TILE
    ns = 2
    tn = n // ns
    if mod.shape[0] == 1:
        mspec = pl.BlockSpec((1, 1, mod.shape[2]), lambda j, bi, i: (0, 0, 0))
    else:
        mspec = pl.BlockSpec((1, 1, mod.shape[2]), lambda j, bi, i: (bi, 0, 0))
    return pl.pallas_call(
        _ffn_up_kernel,
        grid=(ns, b, t // tt),
        in_specs=[
            pl.BlockSpec((1, tt, d), lambda j, bi, i: (bi, i, 0)),
            mspec,
            pl.BlockSpec(nw.shape, lambda j, bi, i: (0, 0)),
            pl.BlockSpec((d, tn), lambda j, bi, i: (0, j)),
        ],
        out_specs=pl.BlockSpec((1, tt, tn), lambda j, bi, i: (bi, i, j)),
        out_shape=jax.ShapeDtypeStruct((b, t, n), F32),
        compiler_params=_params("arbitrary", "arbitrary", "arbitrary"),
        name="ffn_up",
    )(x, mod, nw, w_up)


def _conv_taps(cur, up, down, w_ref, width, vertical):
    n = cur.shape[0]
    col = lax.broadcasted_iota(jnp.int32, (n, 1), 0) % width
    not_first = col != 0
    not_last = col != width - 1
    if vertical:
        ext = jnp.concatenate([up, cur, down], axis=0)
        slabs = [(0, ext[0:n]), (1, cur), (2, ext[2 * width:2 * width + n])]
    else:
        slabs = [(1, cur)]
    acc = None
    for dr, rows in slabs:
        left = jnp.where(not_first, pltpu.roll(rows, 1, axis=0), 0.0)
        right = jnp.where(not_last, pltpu.roll(rows, n - 1, axis=0), 0.0)
        term = (left * w_ref[3 * dr:3 * dr + 1, :] + rows * w_ref[3 * dr + 1:3 * dr + 2, :]
                + right * w_ref[3 * dr + 2:3 * dr + 3, :])
        acc = term if acc is None else acc + term
    return acc


def _ffn_down_kernel(*refs, width, vertical, final_norm):
    if vertical:
        (x_ref, mod_ref, val_ref, vup_ref, vdn_ref, gat_ref, gup_ref, gdn_ref,
         wcv_ref, wcg_ref, bv_ref, bg_ref, wd_ref, nf_ref, o_ref, acc_ref) = refs
    else:
        (x_ref, mod_ref, val_ref, gat_ref,
         wcv_ref, wcg_ref, bv_ref, bg_ref, wd_ref, nf_ref, o_ref, acc_ref) = refs
        vup_ref = vdn_ref = gup_ref = gdn_ref = None
    i = pl.program_id(1)
    c = pl.program_id(2)
    d = x_ref.shape[2]

    def halo(ref, edge):
        if ref is None:
            return None
        return jnp.where(edge, 0.0, ref[0])

    top = i == 0
    bottom = i == pl.num_programs(1) - 1
    val = _conv_taps(val_ref[0], halo(vup_ref, top), halo(vdn_ref, bottom), wcv_ref, width, vertical)
    gat = _conv_taps(gat_ref[0], halo(gup_ref, top), halo(gdn_ref, bottom), wcg_ref, width, vertical)
    val = val + bv_ref[...]
    gat = gat + bg_ref[...]
    act = gat * _sigmoid(gat) * val
    part = _dot(act, wd_ref[...])

    @pl.when(c == 0)
    def _():
        acc_ref[...] = part

    @pl.when(c != 0)
    def _():
        acc_ref[...] += part

    @pl.when(c == pl.num_programs(2) - 1)
    def _():
        gate = mod_ref[0, :, 5 * d:6 * d]
        y = x_ref[0] + gate * acc_ref[...]
        if final_norm:
            ms = jnp.mean(y * y, axis=-1, keepdims=True)
            y = y * lax.rsqrt(ms + RMS_EPS) * nf_ref[...]
        o_ref[0] = y


def _ffn_down(x, up, mod, w_conv, b_conv, w_down, nf, *, width, final_norm):
    b, t, d = x.shape
    f = w_down.shape[0]
    tt = TOKEN_TILE
    nt = t // tt
    vertical = width < t
    ck = f // 2
    nc = f // ck
    wc = w_conv.reshape(9, 2 * f)
    bc = b_conv.reshape(1, 2 * f)
    hb = tt // width if vertical else 1
    nhb = t // width if vertical else 1

    if mod.shape[0] == 1:
        mspec = pl.BlockSpec((1, 1, mod.shape[2]), lambda bi, i, c: (0, 0, 0))
    else:
        mspec = pl.BlockSpec((1, 1, mod.shape[2]), lambda bi, i, c: (bi, 0, 0))
    tok = pl.BlockSpec((1, tt, d), lambda bi, i, c: (bi, i, 0))

    def chan(off):
        main = pl.BlockSpec((1, tt, ck), lambda bi, i, c: (bi, i, c + off))
        if not vertical:
            return [main]
        upb = pl.BlockSpec((1, width, ck), lambda bi, i, c: (bi, jnp.maximum(i * hb - 1, 0), c + off))
        dnb = pl.BlockSpec((1, width, ck), lambda bi, i, c: (bi, jnp.minimum((i + 1) * hb, nhb - 1), c + off))
        return [main, upb, dnb]

    in_specs = ([tok, mspec] + chan(0) + chan(nc) + [
        pl.BlockSpec((9, ck), lambda bi, i, c: (0, c)),
        pl.BlockSpec((9, ck), lambda bi, i, c: (0, c + nc)),
        pl.BlockSpec((1, ck), lambda bi, i, c: (0, c)),
        pl.BlockSpec((1, ck), lambda bi, i, c: (0, c + nc)),
        pl.BlockSpec((ck, d), lambda bi, i, c: (c, 0)),
        pl.BlockSpec(nf.shape, lambda bi, i, c: (0, 0)),
    ])
    ups = [up] * (6 if vertical else 2)
    kern = functools.partial(_ffn_down_kernel, width=width, vertical=vertical, final_norm=final_norm)
    return pl.pallas_call(
        kern,
        grid=(b, nt, nc),
        in_specs=in_specs,
        out_specs=tok,
        out_shape=jax.ShapeDtypeStruct((b, t, d), F32),
        scratch_shapes=[pltpu.VMEM((tt, d), F32)],
        compiler_params=_params("arbitrary", "arbitrary", "arbitrary"),
        name="ffn_down",
    )(x, mod, *ups, wc, wc, bc, bc, w_down, nf)


def _sgu_kernel(x_ref, mod_ref, nw_ref, win_ref, lnw_ref, lnb_ref, ws_ref, bs_ref, wout_ref, o_ref):
    d = x_ref.shape[2]
    tt = x_ref.shape[1]
    e = wout_ref.shape[0]
    groups = ws_ref.shape[0]
    gw = e // groups
    shift = mod_ref[0, :, 0:d]
    scale = mod_ref[0, :, d:2 * d]
    gate = mod_ref[0, :, 2 * d:3 * d]
    x = x_ref[0]
    h = _norm_mod(x, nw_ref[...], shift, scale)
    z = _dot(h, win_ref[...])
    z = 0.5 * z * (1.0 + lax.erf(z * (1.0 / math.sqrt(2.0))))
    u = z[:, :e]
    v = z[:, e:]
    mu = jnp.mean(v, axis=-1, keepdims=True)
    vc = v - mu
    var = jnp.mean(vc * vc, axis=-1, keepdims=True)
    vn = (vc * lax.rsqrt(var + LN_EPS) * lnw_ref[...] + lnb_ref[...]).astype(BF16)
    chunks = []
    for p in range(tt // SGU_CHUNK):
        rows = slice(p * SGU_CHUNK, (p + 1) * SGU_CHUNK)
        parts = [jnp.dot(ws_ref[gi], vn[rows, gi * gw:(gi + 1) * gw], preferred_element_type=F32)
                 for gi in range(groups)]
        vm = jnp.concatenate(parts, axis=1) + bs_ref[...]
        chunks.append(u[rows] * vm)
    gated = jnp.concatenate(chunks, axis=0)
    out = _dot(gated, wout_ref[...])
    o_ref[0] = x + gate * out


def _sgu(x, mod, nw, w_in, ln_w, ln_b, w_s, b_full, w_out):
    b, t, d = x.shape
    tt = TOKEN_TILE
    tok = pl.BlockSpec((1, tt, d), lambda bi, i: (bi, i, 0))
    consts = (nw, w_in, ln_w, ln_b, w_s, b_full, w_out)
    return pl.pallas_call(
        _sgu_kernel,
        grid=(b, t // tt),
        in_specs=[tok, _mod_spec(mod)] + [_const_spec(c) for c in consts],
        out_specs=tok,
        out_shape=jax.ShapeDtypeStruct((b, t, d), F32),
        compiler_params=_params("arbitrary", "arbitrary"),
        name="sgu",
    )(x, mod, *consts)


def _to_scan(xf, xb, heads):
    b, t, _ = xf.shape
    g = b // SCAN_BATCH

    def tr(x):
        x = x.reshape(g, SCAN_BATCH, t, heads, HEAD_DIM)
        return x.transpose(0, 2, 4, 1, 3).reshape(g, t, HEAD_DIM, SCAN_BATCH * heads)

    return jnp.concatenate([tr(xf), tr(xb)[:, ::-1]], axis=-1)


def _from_scan(o, heads):
    g, th, hd, _ = o.shape
    half = SCAN_BATCH * heads

    def tr(x):
        x = x.reshape(g, th, hd, SCAN_BATCH, heads)
        return x.transpose(0, 3, 1, 4, 2)

    lo = tr(o[..., :half])
    hi = tr(o[..., half:])[:, :, ::-1]
    return jnp.concatenate([lo, hi], axis=2).reshape(g * SCAN_BATCH, 2 * th, heads * hd)


def _lane_param(p, heads):
    m = p.reshape(heads, HEAD_DIM).T
    return jnp.tile(m, (1, LANES // heads))


def _state_to_scan(sf, sb):
    def tr(s):
        b, h = s.shape[:2]
        s = s.reshape(b // SCAN_BATCH, SCAN_BATCH, h, HEAD_DIM, HEAD_DIM)
        return s.transpose(0, 4, 3, 1, 2).reshape(b // SCAN_BATCH, HEAD_DIM, HEAD_DIM, SCAN_BATCH * h)

    return jnp.concatenate([tr(sf), tr(sb)], axis=-1)


def _state_from_scan(s, heads):
    g = s.shape[0]
    half = SCAN_BATCH * heads

    def tr(x):
        x = x.reshape(g, HEAD_DIM, HEAD_DIM, SCAN_BATCH, heads)
        return x.transpose(0, 3, 4, 2, 1).reshape(g * SCAN_BATCH, heads, HEAD_DIM, HEAD_DIM)

    return tr(s[..., :half]), tr(s[..., half:])


def kernel(x_prompt, x_sample, state_ctx_fwd, state_ctx_bwd, c, c_ctx, ada_w, ada_b, norm_mix, norm_ffn, ffn_up, ffn_conv, ffn_conv_b, ffn_down, norm_final, rw_mu, rw_wr, rw_wk, rw_wv, rw_wo, rw_w0, rw_w1, rw_w2, rw_a0, rw_a1, rw_a2, rw_g1, rw_g2, rw_kk, rw_ka, rw_rk, rw_lnx_w, rw_lnx_b, sg_in, sg_ln_w, sg_ln_b, sg_ws, sg_bs, sg_out):
    depth, d, _ = ada_w.shape
    heads = d // HEAD_DIM
    n_dec = c.shape[0]
    assert SCAN_BATCH * heads * 2 == LANES
    assert n_dec + 1 <= SUBLANES

    cond8 = jnp.concatenate(
        [c_ctx[None, :], c, jnp.zeros((SUBLANES - 1 - n_dec, d), F32)], axis=0)
    mod_all = _modulation(cond8, ada_w, ada_b)
    nf = norm_final.reshape(1, d)

    def run_stream(x, mod_rows, s0f, s0b, width, want_state):
        b, t, _ = x.shape
        new_f, new_b = [], []
        for i in range(depth):
            mod = mod_all[i, mod_rows[0]:mod_rows[1]][:, None, :]
            j = i // 2
            nw_mix = norm_mix[i].reshape(1, d)
            if i % 2 == 0:
                w1 = jnp.concatenate([rw_w1[j, 0], rw_w1[j, 1]], axis=1).astype(BF16)
                a1 = jnp.concatenate([rw_a1[j, 0], rw_a1[j, 1]], axis=1).astype(BF16)
                r, k, v, df, db, af, ab, g = _rw_proj(
                    x, mod, nw_mix, rw_mu[j],
                    rw_wr[j].astype(BF16), rw_wk[j].astype(BF16), rw_wv[j].astype(BF16),
                    w1, a1, rw_g1[j].astype(BF16),
                    rw_w2[j].astype(BF16), rw_a2[j].astype(BF16), rw_w0[j], rw_a0[j])
                rs = _to_scan(r, r, heads)
                ks = _to_scan(k, k, heads)
                vs = _to_scan(v, v, heads)
                ws = _to_scan(df, db, heads)
                as_ = _to_scan(af, ab, heads)
                s0 = _state_to_scan(s0f[:, j], s0b[:, j])
                kk_l = _lane_param(rw_kk[j], heads)
                ka_l = _lane_param(rw_ka[j], heads)
                ys, sfin = _scan(rs, ks, vs, ws, as_, s0, kk_l, ka_l)
                yn = _rw_post(ys, rs, ks, vs, as_, ka_l,
                              _lane_param(rw_rk[j].reshape(-1), heads),
                              _lane_param(rw_lnx_w[j], heads), _lane_param(rw_lnx_b[j], heads))
                yn = _from_scan(yn, heads)
                x = _rw_out(x, yn, g, mod, rw_g2[j].astype(BF16), rw_wo[j].astype(BF16))
                if want_state:
                    sf, sb = _state_from_scan(sfin, heads)
                    new_f.append(sf)
                    new_b.append(sb)
            else:
                e = sg_out.shape[1]
                b_full = jnp.repeat(sg_bs[j].T, e // sg_bs.shape[1], axis=1)
                x = _sgu(x, mod, nw_mix, sg_in[j].astype(BF16),
                         sg_ln_w[j].reshape(1, e), sg_ln_b[j].reshape(1, e),
                         sg_ws[j].astype(BF16), b_full, sg_out[j].astype(BF16))
            up = _ffn_up(x, mod, norm_ffn[i].reshape(1, d), ffn_up[i].astype(BF16))
            x = _ffn_down(x, up, mod, ffn_conv[i], ffn_conv_b[i], ffn_down[i].astype(BF16), nf,
                          width=width, final_norm=(i == depth - 1))
        if want_state:
            return x, jnp.stack(new_f, axis=1), jnp.stack(new_b, axis=1)
        return x, None, None

    bp = x_prompt.shape[0]
    n_rwkv = state_ctx_fwd.shape[1]
    zero_state = jnp.zeros((bp, n_rwkv, heads, HEAD_DIM, HEAD_DIM), F32)
    y_prompt, new_f, new_b = run_stream(x_prompt, (0, 1), zero_state, zero_state,
                                        x_prompt.shape[1], True)
    y_sample, _, _ = run_stream(x_sample, (1, 1 + n_dec), state_ctx_fwd, state_ctx_bwd,
                                GRID_W, False)
    return (y_prompt, y_sample, new_f, new_b)
```

```python
import functools
import math

import jax
import jax.numpy as jnp
from jax import lax
from jax.experimental import pallas as pl
from jax.experimental.pallas import tpu as pltpu

RMS_EPS = 1e-6
LN_EPS = 1e-5
GN_EPS = 64e-5
HEAD_DIM = 64
SGU_CHUNK = 128
GRID_W = 64

LANES = 128
SUBLANES = 8
VMEM_LIMIT = 56 * 1024 * 1024

TOKEN_TILE = 256
SCAN_BATCH = 4
SCAN_TCHUNK = 32
SCAN_ROWS = 32
SCAN_ACCS = 2
SCAN_SLOTS = 3
POST_TCHUNK = 16

BF16 = jnp.bfloat16
F32 = jnp.float32


def _params(*sem):
    return pltpu.CompilerParams(dimension_semantics=sem, vmem_limit_bytes=VMEM_LIMIT)


def _dot(a, b):
    return jnp.dot(a.astype(BF16), b, preferred_element_type=F32)


def _norm_mod(x, g, shift, scale):
    ms = jnp.mean(x * x, axis=-1, keepdims=True)
    return x * lax.rsqrt(ms + RMS_EPS) * g * (1.0 + scale) + shift


def _sigmoid(x):
    return 1.0 / (1.0 + jnp.exp(-x))


def _tree_sum(xs):
    while len(xs) > 1:
        xs = [xs[i] + xs[i + 1] for i in range(0, len(xs), 2)]
    return xs[0]


def _mod_kernel(c_ref, w_ref, b_ref, o_ref):
    c = c_ref[...]
    sc = c * _sigmoid(c)
    o_ref[0] = jnp.dot(sc, w_ref[0], preferred_element_type=F32,
                       precision=lax.Precision.HIGHEST) + b_ref[0]


def _modulation(cond8, ada_w, ada_b):
    depth, d, n = ada_w.shape
    tn = n // 4
    return pl.pallas_call(
        _mod_kernel,
        grid=(depth, n // tn),
        in_specs=[
            pl.BlockSpec((SUBLANES, d), lambda l, j: (0, 0)),
            pl.BlockSpec((1, d, tn), lambda l, j: (l, 0, j)),
            pl.BlockSpec((1, 1, tn), lambda l, j: (l, 0, j)),
        ],
        out_specs=pl.BlockSpec((1, SUBLANES, tn), lambda l, j: (l, 0, j)),
        out_shape=jax.ShapeDtypeStruct((depth, SUBLANES, n), F32),
        compiler_params=_params("arbitrary", "arbitrary"),
        name="modulation",
    )(cond8, ada_w, ada_b.reshape(depth, 1, n))


def _mod_spec(mod):
    if mod.shape[0] == 1:
        return pl.BlockSpec((1, 1, mod.shape[2]), lambda b, i: (0, 0, 0))
    return pl.BlockSpec((1, 1, mod.shape[2]), lambda b, i: (b, 0, 0))


def _const_spec(a):
    nd = a.ndim
    return pl.BlockSpec(a.shape, lambda *_: (0,) * nd)


def _rw_proj_kernel(x_ref, xp_ref, xn_ref, mod_ref, nw_ref, mu_ref,
                    wr_ref, wk_ref, wv_ref, w1_ref, a1_ref, g1_ref,
                    w2_ref, a2_ref, w0_ref, a0_ref,
                    r_ref, k_ref, v_ref, df_ref, db_ref, af_ref, ab_ref, g_ref):
    i = pl.program_id(1)
    last = pl.num_programs(1) - 1
    d = x_ref.shape[2]
    tt = x_ref.shape[1]
    nw = nw_ref[...]
    shift = mod_ref[0, :, 0:d]
    scale = mod_ref[0, :, d:2 * d]
    h = _norm_mod(x_ref[0], nw, shift, scale)
    hp = _norm_mod(xp_ref[0], nw, shift, scale)[SUBLANES - 1:SUBLANES, :]
    hn = _norm_mod(xn_ref[0], nw, shift, scale)[0:1, :]
    hp = jnp.where(i == 0, 0.0, hp)
    hn = jnp.where(i == last, 0.0, hn)
    rows = lax.broadcasted_iota(jnp.int32, (tt, 1), 0)
    prev = jnp.where(rows == 0, hp, pltpu.roll(h, 1, axis=0))
    nxt = jnp.where(rows == tt - 1, hn, pltpu.roll(h, tt - 1, axis=0))
    dp = prev - h
    dn = nxt - h

    def shifted(idx):
        return h + dp * mu_ref[0, idx:idx + 1, :] + dn * mu_ref[1, idx:idx + 1, :]

    r_ref[0] = _dot(shifted(0), wr_ref[...])
    k_ref[0] = _dot(shifted(2), wk_ref[...])
    v_ref[0] = _dot(shifted(3), wv_ref[...])
    g_ref[0] = _sigmoid(_dot(shifted(5), g1_ref[...]))
    lw = jnp.tanh(_dot(shifted(1), w1_ref[...]))
    la = _dot(shifted(4), a1_ref[...])
    lora = w2_ref.shape[1]
    decay_scale = math.exp(-0.5)
    for dirn, (d_ref, a_ref) in enumerate(((df_ref, af_ref), (db_ref, ab_ref))):
        w_raw = w0_ref[dirn:dirn + 1, :] + _dot(lw[:, dirn * lora:(dirn + 1) * lora], w2_ref[dirn])
        d_ref[0] = jnp.exp(-decay_scale * _sigmoid(w_raw))
        a_raw = a0_ref[dirn:dirn + 1, :] + _dot(la[:, dirn * lora:(dirn + 1) * lora], a2_ref[dirn])
        a_ref[0] = _sigmoid(a_raw)


def _rw_proj(x, mod, nw, mu, wr, wk, wv, w1, a1, g1, w2, a2, w0, a0):
    b, t, d = x.shape
    tt = TOKEN_TILE
    nt = t // tt
    hb = tt // SUBLANES
    nhb = t // SUBLANES
    tok = pl.BlockSpec((1, tt, d), lambda bi, i: (bi, i, 0))
    prev = pl.BlockSpec((1, SUBLANES, d), lambda bi, i: (bi, jnp.maximum(i * hb - 1, 0), 0))
    nxt = pl.BlockSpec((1, SUBLANES, d), lambda bi, i: (bi, jnp.minimum((i + 1) * hb, nhb - 1), 0))
    consts = (nw, mu, wr, wk, wv, w1, a1, g1, w2, a2, w0, a0)
    gl = g1.shape[1]
    out_shape = [jax.ShapeDtypeStruct((b, t, d), F32)] * 7 + [jax.ShapeDtypeStruct((b, t, gl), F32)]
    out_specs = [tok] * 7 + [pl.BlockSpec((1, tt, gl), lambda bi, i: (bi, i, 0))]
    return pl.pallas_call(
        _rw_proj_kernel,
        grid=(b, nt),
        in_specs=[tok, prev, nxt, _mod_spec(mod)] + [_const_spec(c) for c in consts],
        out_specs=out_specs,
        out_shape=out_shape,
        compiler_params=_params("arbitrary", "arbitrary"),
        name="rw_proj",
    )(x, x, x, mod, *consts)


def _scan_kernel(rf_ref, rm_ref, kf_ref, km_ref, vf_ref, vm_ref, wf_ref, wm_ref, af_ref, am_ref,
                 s0_ref, kk_ref, ka_ref, rk_ref, y_ref, bv_ref, sfin_ref, s_ref, ops_ref, sa_ref):
    c = pl.program_id(1)
    nb, tc, _, hd = rf_ref.shape

    @pl.when(c == 0)
    def _():
        s_ref[...] = s0_ref[0]

    def prepare(s, slot):
        m = tc - 1 - s

        def chains(fwd_ref, bwd_ref):
            parts = [fwd_ref[b, s] for b in range(nb)] + [bwd_ref[b, m] for b in range(nb)]
            return jnp.concatenate(parts, axis=0).T

        r = chains(rf_ref, rm_ref)
        kraw = chains(kf_ref, km_ref)
        v = chains(vf_ref, vm_ref)
        a = chains(af_ref, am_ref)
        kkv = kraw * kk_ref[...]
        nrm = jnp.sqrt(jnp.sum(kkv * kkv, axis=0, keepdims=True))
        kk = kkv / jnp.maximum(nrm, 1e-12)
        kd = kraw * (1.0 + (a - 1.0) * ka_ref[...])
        ops_ref[slot, 0] = -kk
        ops_ref[slot, 1] = kk * a
        ops_ref[slot, 2] = kd
        ops_ref[slot, 3] = chains(wf_ref, wm_ref)
        ops_ref[slot, 4] = r
        ops_ref[slot, 5] = v
        bv_ref[0, s] = v * jnp.sum(r * kd * rk_ref[...], axis=0, keepdims=True)

    def accumulate(acc, j, term):
        n = j % SCAN_ACCS
        acc[n] = term if acc[n] is None else acc[n] + term

    def first_sa(slot):
        for i0 in range(0, hd, SCAN_ROWS):
            rows = slice(i0, i0 + SCAN_ROWS)
            acc = [None] * SCAN_ACCS
            for j in range(hd):
                accumulate(acc, j, s_ref[j, rows, :] * ops_ref[slot, 0, j:j + 1, :])
            sa_ref[rows, :] = _tree_sum(acc)

    def advance(s, slot, nxt):
        for i0 in range(0, hd, SCAN_ROWS):
            rows = slice(i0, i0 + SCAN_ROWS)
            sa = sa_ref[rows, :]
            vr = ops_ref[slot, 5, rows, :]
            acc_y = [None] * SCAN_ACCS
            acc_a = [None] * SCAN_ACCS
            for j in range(hd):
                sn = (s_ref[j, rows, :] * ops_ref[slot, 3, j:j + 1, :]
                      + sa * ops_ref[slot, 1, j:j + 1, :] + vr * ops_ref[slot, 2, j:j + 1, :])
                s_ref[j, rows, :] = sn
                accumulate(acc_y, j, sn * ops_ref[slot, 4, j:j + 1, :])
                accumulate(acc_a, j, sn * ops_ref[nxt, 0, j:j + 1, :])
            y_ref[0, s, rows, :] = _tree_sum(acc_y)
            sa_ref[rows, :] = _tree_sum(acc_a)

    prepare(0, 0)
    prepare(1, 1)
    first_sa(0)

    def step(s, carry):
        slot = s % SCAN_SLOTS
        nxt = (s + 1) % SCAN_SLOTS
        advance(s, slot, nxt)
        prepare(jnp.minimum(s + 2, tc - 1), (s + 2) % SCAN_SLOTS)
        return carry

    lax.fori_loop(0, tc, step, 0)

    @pl.when(c == pl.num_programs(1) - 1)
    def _():
        sfin_ref[0] = s_ref[...]


def _scan(r, k, v, wf, wb, af, ab, s0, kk, ka, rk):
    b, t, heads, hd = r.shape
    g = b // SCAN_BATCH
    tc = SCAN_TCHUNK
    nt = t // tc
    fwd = pl.BlockSpec((SCAN_BATCH, tc, heads, hd), lambda gi, c: (gi, c, 0, 0))
    bwd = pl.BlockSpec((SCAN_BATCH, tc, heads, hd), lambda gi, c: (gi, nt - 1 - c, 0, 0))
    step = pl.BlockSpec((1, tc, hd, LANES), lambda gi, c: (gi, c, 0, 0))
    state = pl.BlockSpec((1, hd, hd, LANES), lambda gi, c: (gi, 0, 0, 0))
    steps = jax.ShapeDtypeStruct((g, t, hd, LANES), F32)
    return pl.pallas_call(
        _scan_kernel,
        grid=(g, nt),
        in_specs=[fwd, bwd] * 5 + [state, _const_spec(kk), _const_spec(ka), _const_spec(rk)],
        out_specs=[step, step, state],
        out_shape=[steps, steps, jax.ShapeDtypeStruct((g, hd, hd, LANES), F32)],
        scratch_shapes=[pltpu.VMEM((hd, hd, LANES), F32),
                        pltpu.VMEM((SCAN_SLOTS, 6, hd, LANES), F32),
                        pltpu.VMEM((hd, LANES), F32)],
        compiler_params=_params("arbitrary", "arbitrary"),
        name="wkv_scan",
    )(r, r, k, k, v, v, wf, wb, af, ab, s0, kk, ka, rk)


def _rw_post_kernel(y_ref, ym_ref, bv_ref, bvm_ref, lw_ref, lb_ref, lo_ref, hi_ref):
    tp = y_ref.shape[1]
    nb, _, heads, _ = lo_ref.shape
    half = LANES // 2

    def step(t, carry):
        m = tp - 1 - t
        y = y_ref[0, t] + pltpu.roll(ym_ref[0, m], half, axis=1)
        bonus = bv_ref[0, t] + pltpu.roll(bvm_ref[0, m], half, axis=1)
        mean = jnp.mean(y, axis=0, keepdims=True)
        yc = y - mean
        var = jnp.mean(yc * yc, axis=0, keepdims=True)
        yn = yc * lax.rsqrt(var + GN_EPS) * lw_ref[...] + lb_ref[...]
        rows = (yn + bonus).T
        for b in range(nb):
            lo_ref[b, t] = rows[b * heads:(b + 1) * heads]
            hi_ref[b, m] = rows[half + b * heads:half + (b + 1) * heads]
        return carry

    lax.fori_loop(0, tp, step, 0)


def _rw_post(y, bv, lw, lb, heads):
    g, t, hd, _ = y.shape
    tp = POST_TCHUNK
    nb = t // tp
    cur = pl.BlockSpec((1, tp, hd, LANES), lambda gi, c: (gi, c, 0, 0))
    mir = pl.BlockSpec((1, tp, hd, LANES), lambda gi, c: (gi, nb - 1 - c, 0, 0))
    lo = pl.BlockSpec((SCAN_BATCH, tp, heads, hd), lambda gi, c: (gi, c, 0, 0))
    hi = pl.BlockSpec((SCAN_BATCH, tp, heads, hd), lambda gi, c: (gi, nb // 2 - 1 - c, 0, 0))
    half_tokens = jax.ShapeDtypeStruct((g * SCAN_BATCH, t // 2, heads, hd), F32)
    return pl.pallas_call(
        _rw_post_kernel,
        grid=(g, nb // 2),
        in_specs=[cur, mir, cur, mir, _const_spec(lw), _const_spec(lb)],
        out_specs=[lo, hi],
        out_shape=[half_tokens, half_tokens],
        compiler_params=_params("arbitrary", "arbitrary"),
        name="rw_post",
    )(y, y, bv, bv, lw, lb)


def _rw_out_kernel(x_ref, yn_ref, g_ref, mod_ref, g2_ref, wo_ref, o_ref):
    d = x_ref.shape[2]
    gate = mod_ref[0, :, 2 * d:3 * d]
    g = _dot(g_ref[0], g2_ref[...])
    out = _dot(yn_ref[0] * g, wo_ref[...])
    o_ref[0] = x_ref[0] + gate * out


def _rw_out(x, yn, g, mod, g2, wo):
    b, t, d = x.shape
    tt = TOKEN_TILE
    tok = pl.BlockSpec((1, tt, d), lambda bi, i: (bi, i, 0))
    gspec = pl.BlockSpec((1, tt, g.shape[2]), lambda bi, i: (bi, i, 0))
    return pl.pallas_call(
        _rw_out_kernel,
        grid=(b, t // tt),
        in_specs=[tok, tok, gspec, _mod_spec(mod), _const_spec(g2), _const_spec(wo)],
        out_specs=tok,
        out_shape=jax.ShapeDtypeStruct((b, t, d), F32),
        compiler_params=_params("arbitrary", "arbitrary"),
        name="rw_out",
    )(x, yn, g, mod, g2, wo)


def _ffn_up_kernel(x_ref, mod_ref, nw_ref, w_ref, o_ref):
    d = x_ref.shape[2]
    shift = mod_ref[0, :, 3 * d:4 * d]
    scale = mod_ref[0, :, 4 * d:5 * d]
    h = _norm_mod(x_ref[0], nw_ref[...], shift, scale)
    o_ref[0] = _dot(h, w_ref[...])


def _ffn_up(x, mod, nw, w_up):
    b, t, d = x.shape
    n = w_up.shape[1]
    tt = TOKEN_TILE
    ns = 2
    tn = n // ns
    if mod.shape[0] == 1:
        mspec = pl.BlockSpec((1, 1, mod.shape[2]), lambda j, bi, i: (0, 0, 0))
    else:
        mspec = pl.BlockSpec((1, 1, mod.shape[2]), lambda j, bi, i: (bi, 0, 0))
    return pl.pallas_call(
        _ffn_up_kernel,
        grid=(ns, b, t // tt),
        in_specs=[
            pl.BlockSpec((1, tt, d), lambda j, bi, i: (bi, i, 0)),
            mspec,
            pl.BlockSpec(nw.shape, lambda j, bi, i: (0, 0)),
            pl.BlockSpec((d, tn), lambda j, bi, i: (0, j)),
        ],
        out_specs=pl.BlockSpec((1, tt, tn), lambda j, bi, i: (bi, i, j)),
        out_shape=jax.ShapeDtypeStruct((b, t, n), F32),
        compiler_params=_params("arbitrary", "arbitrary", "arbitrary"),
        name="ffn_up",
    )(x, mod, nw, w_up)


def _conv_taps(cur, up, down, w_ref, width, vertical):
    n = cur.shape[0]
    col = lax.broadcasted_iota(jnp.int32, (n, 1), 0) % width
    if vertical:
        ext = jnp.concatenate([up, cur, down], axis=0)
        slabs = [(0, ext[0:n]), (1, cur), (2, ext[2 * width:2 * width + n])]
    else:
        slabs = [(1, cur)]
    z = []
    for dc in range(3):
        acc = None
        for dr, rows in slabs:
            term = rows * w_ref[3 * dr + dc:3 * dr + dc + 1, :]
            acc = term if acc is None else acc + term
        z.append(acc)
    left = jnp.where(col != 0, pltpu.roll(z[0], 1, axis=0), 0.0)
    right = jnp.where(col != width - 1, pltpu.roll(z[2], n - 1, axis=0), 0.0)
    return left + z[1] + right


def _ffn_down_kernel(*refs, width, vertical, final_norm):
    if vertical:
        (x_ref, mod_ref, val_ref, vup_ref, vdn_ref, gat_ref, gup_ref, gdn_ref,
         wcv_ref, wcg_ref, bv_ref, bg_ref, wd_ref, nf_ref, o_ref, acc_ref) = refs
    else:
        (x_ref, mod_ref, val_ref, gat_ref,
         wcv_ref, wcg_ref, bv_ref, bg_ref, wd_ref, nf_ref, o_ref, acc_ref) = refs
        vup_ref = vdn_ref = gup_ref = gdn_ref = None
    i = pl.program_id(1)
    c = pl.program_id(2)
    d = x_ref.shape[2]

    def halo(ref, edge):
        if ref is None:
            return None
        return jnp.where(edge, 0.0, ref[0])

    top = i == 0
    bottom = i == pl.num_programs(1) - 1
    val = _conv_taps(val_ref[0], halo(vup_ref, top), halo(vdn_ref, bottom), wcv_ref, width, vertical)
    gat = _conv_taps(gat_ref[0], halo(gup_ref, top), halo(gdn_ref, bottom), wcg_ref, width, vertical)
    val = val + bv_ref[...]
    gat = gat + bg_ref[...]
    act = gat * _sigmoid(gat) * val
    part = _dot(act, wd_ref[...])

    @pl.when(c == 0)
    def _():
        acc_ref[...] = part

    @pl.when(c != 0)
    def _():
        acc_ref[...] += part

    @pl.when(c == pl.num_programs(2) - 1)
    def _():
        gate = mod_ref[0, :, 5 * d:6 * d]
        y = x_ref[0] + gate * acc_ref[...]
        if final_norm:
            ms = jnp.mean(y * y, axis=-1, keepdims=True)
            y = y * lax.rsqrt(ms + RMS_EPS) * nf_ref[...]
        o_ref[0] = y


def _ffn_down(x, up, mod, w_conv, b_conv, w_down, nf, *, width, final_norm):
    b, t, d = x.shape
    f = w_down.shape[0]
    tt = TOKEN_TILE
    nt = t // tt
    vertical = width < t
    ck = f // 2
    nc = f // ck
    wc = w_conv.reshape(9, 2 * f)
    bc = b_conv.reshape(1, 2 * f)
    hb = tt // width if vertical else 1
    nhb = t // width if vertical else 1

    if mod.shape[0] == 1:
        mspec = pl.BlockSpec((1, 1, mod.shape[2]), lambda bi, i, c: (0, 0, 0))
    else:
        mspec = pl.BlockSpec((1, 1, mod.shape[2]), lambda bi, i, c: (bi, 0, 0))
    tok = pl.BlockSpec((1, tt, d), lambda bi, i, c: (bi, i, 0))

    def chan(off):
        main = pl.BlockSpec((1, tt, ck), lambda bi, i, c: (bi, i, c + off))
        if not vertical:
            return [main]
        upb = pl.BlockSpec((1, width, ck), lambda bi, i, c: (bi, jnp.maximum(i * hb - 1, 0), c + off))
        dnb = pl.BlockSpec((1, width, ck), lambda bi, i, c: (bi, jnp.minimum((i + 1) * hb, nhb - 1), c + off))
        return [main, upb, dnb]

    in_specs = ([tok, mspec] + chan(0) + chan(nc) + [
        pl.BlockSpec((9, ck), lambda bi, i, c: (0, c)),
        pl.BlockSpec((9, ck), lambda bi, i, c: (0, c + nc)),
        pl.BlockSpec((1, ck), lambda bi, i, c: (0, c)),
        pl.BlockSpec((1, ck), lambda bi, i, c: (0, c + nc)),
        pl.BlockSpec((ck, d), lambda bi, i, c: (c, 0)),
        pl.BlockSpec(nf.shape, lambda bi, i, c: (0, 0)),
    ])
    ups = [up] * (6 if vertical else 2)
    kern = functools.partial(_ffn_down_kernel, width=width, vertical=vertical, final_norm=final_norm)
    return pl.pallas_call(
        kern,
        grid=(b, nt, nc),
        in_specs=in_specs,
        out_specs=tok,
        out_shape=jax.ShapeDtypeStruct((b, t, d), F32),
        scratch_shapes=[pltpu.VMEM((tt, d), F32)],
        compiler_params=_params("arbitrary", "arbitrary", "arbitrary"),
        name="ffn_down",
    )(x, mod, *ups, wc, wc, bc, bc, w_down, nf)


def _sgu_kernel(x_ref, mod_ref, nw_ref, win_ref, lnw_ref, lnb_ref, ws_ref, bs_ref, wout_ref, o_ref):
    d = x_ref.shape[2]
    tt = x_ref.shape[1]
    e = wout_ref.shape[0]
    groups = ws_ref.shape[0]
    gw = e // groups
    shift = mod_ref[0, :, 0:d]
    scale = mod_ref[0, :, d:2 * d]
    gate = mod_ref[0, :, 2 * d:3 * d]
    x = x_ref[0]
    h = _norm_mod(x, nw_ref[...], shift, scale)
    z = _dot(h, win_ref[...])
    z = 0.5 * z * (1.0 + lax.erf(z * (1.0 / math.sqrt(2.0))))
    u = z[:, :e]
    v = z[:, e:]
    mu = jnp.mean(v, axis=-1, keepdims=True)
    vc = v - mu
    var = jnp.mean(vc * vc, axis=-1, keepdims=True)
    vn = (vc * lax.rsqrt(var + LN_EPS) * lnw_ref[...] + lnb_ref[...]).astype(BF16)
    chunks = []
    for p in range(tt // SGU_CHUNK):
        rows = slice(p * SGU_CHUNK, (p + 1) * SGU_CHUNK)
        parts = [jnp.dot(ws_ref[gi], vn[rows, gi * gw:(gi + 1) * gw], preferred_element_type=F32)
                 for gi in range(groups)]
        vm = jnp.concatenate(parts, axis=1) + bs_ref[...]
        chunks.append(u[rows] * vm)
    gated = jnp.concatenate(chunks, axis=0)
    out = _dot(gated, wout_ref[...])
    o_ref[0] = x + gate * out


def _sgu(x, mod, nw, w_in, ln_w, ln_b, w_s, b_full, w_out):
    b, t, d = x.shape
    tt = TOKEN_TILE
    tok = pl.BlockSpec((1, tt, d), lambda bi, i: (bi, i, 0))
    consts = (nw, w_in, ln_w, ln_b, w_s, b_full, w_out)
    return pl.pallas_call(
        _sgu_kernel,
        grid=(b, t // tt),
        in_specs=[tok, _mod_spec(mod)] + [_const_spec(c) for c in consts],
        out_specs=tok,
        out_shape=jax.ShapeDtypeStruct((b, t, d), F32),
        compiler_params=_params("arbitrary", "arbitrary"),
        name="sgu",
    )(x, mod, *consts)


def _lane_param(p, heads):
    m = p.reshape(heads, HEAD_DIM).T
    return jnp.tile(m, (1, LANES // heads))


def _state_to_scan(sf, sb):
    def tr(s):
        b, h = s.shape[:2]
        s = s.reshape(b // SCAN_BATCH, SCAN_BATCH, h, HEAD_DIM, HEAD_DIM)
        return s.transpose(0, 4, 3, 1, 2).reshape(b // SCAN_BATCH, HEAD_DIM, HEAD_DIM, SCAN_BATCH * h)

    return jnp.concatenate([tr(sf), tr(sb)], axis=-1)


def _state_from_scan(s, heads):
    g = s.shape[0]
    half = SCAN_BATCH * heads

    def tr(x):
        x = x.reshape(g, HEAD_DIM, HEAD_DIM, SCAN_BATCH, heads)
        return x.transpose(0, 3, 4, 2, 1).reshape(g * SCAN_BATCH, heads, HEAD_DIM, HEAD_DIM)

    return tr(s[..., :half]), tr(s[..., half:])


def kernel(x_prompt, x_sample, state_ctx_fwd, state_ctx_bwd, c, c_ctx, ada_w, ada_b, norm_mix, norm_ffn, ffn_up, ffn_conv, ffn_conv_b, ffn_down, norm_final, rw_mu, rw_wr, rw_wk, rw_wv, rw_wo, rw_w0, rw_w1, rw_w2, rw_a0, rw_a1, rw_a2, rw_g1, rw_g2, rw_kk, rw_ka, rw_rk, rw_lnx_w, rw_lnx_b, sg_in, sg_ln_w, sg_ln_b, sg_ws, sg_bs, sg_out):
    depth, d, _ = ada_w.shape
    heads = d // HEAD_DIM
    n_dec = c.shape[0]
    assert SCAN_BATCH * heads * 2 == LANES
    assert n_dec + 1 <= SUBLANES

    cond8 = jnp.concatenate(
        [c_ctx[None, :], c, jnp.zeros((SUBLANES - 1 - n_dec, d), F32)], axis=0)
    mod_all = _modulation(cond8, ada_w, ada_b)
    nf = norm_final.reshape(1, d)

    def run_stream(x, mod_rows, s0f, s0b, width, want_state):
        b, t, _ = x.shape
        new_f, new_b = [], []
        for i in range(depth):
            mod = mod_all[i, mod_rows[0]:mod_rows[1]][:, None, :]
            j = i // 2
            nw_mix = norm_mix[i].reshape(1, d)
            if i % 2 == 0:
                w1 = jnp.concatenate([rw_w1[j, 0], rw_w1[j, 1]], axis=1).astype(BF16)
                a1 = jnp.concatenate([rw_a1[j, 0], rw_a1[j, 1]], axis=1).astype(BF16)
                r, k, v, df, db, af, ab, g = _rw_proj(
                    x, mod, nw_mix, rw_mu[j],
                    rw_wr[j].astype(BF16), rw_wk[j].astype(BF16), rw_wv[j].astype(BF16),
                    w1, a1, rw_g1[j].astype(BF16),
                    rw_w2[j].astype(BF16), rw_a2[j].astype(BF16), rw_w0[j], rw_a0[j])
                per_head = lambda z: z.reshape(b, t, heads, HEAD_DIM)
                s0 = _state_to_scan(s0f[:, j], s0b[:, j])
                ys, bvs, sfin = _scan(
                    per_head(r), per_head(k), per_head(v), per_head(df), per_head(db),
                    per_head(af), per_head(ab), s0,
                    _lane_param(rw_kk[j], heads), _lane_param(rw_ka[j], heads),
                    _lane_param(rw_rk[j].reshape(-1), heads))
                lo, hi = _rw_post(ys, bvs, _lane_param(rw_lnx_w[j], heads),
                                  _lane_param(rw_lnx_b[j], heads), heads)
                yn = jnp.concatenate([lo, hi], axis=1).reshape(b, t, d)
                x = _rw_out(x, yn, g, mod, rw_g2[j].astype(BF16), rw_wo[j].astype(BF16))
                if want_state:
                    sf, sb = _state_from_scan(sfin, heads)
                    new_f.append(sf)
                    new_b.append(sb)
            else:
                e = sg_out.shape[1]
                b_full = jnp.repeat(sg_bs[j].T, e // sg_bs.shape[1], axis=1)
                x = _sgu(x, mod, nw_mix, sg_in[j].astype(BF16),
                         sg_ln_w[j].reshape(1, e), sg_ln_b[j].reshape(1, e),
                         sg_ws[j].astype(BF16), b_full, sg_out[j].astype(BF16))
            up = _ffn_up(x, mod, norm_ffn[i].reshape(1, d), ffn_up[i].astype(BF16))
            x = _ffn_down(x, up, mod, ffn_conv[i], ffn_conv_b[i], ffn_down[i].astype(BF16), nf,
                          width=width, final_norm=(i == depth - 1))
        if want_state:
            return x, jnp.stack(new_f, axis=1), jnp.stack(new_b, axis=1)
        return x, None, None

    bp = x_prompt.shape[0]
    n_rwkv = state_ctx_fwd.shape[1]
    zero_state = jnp.zeros((bp, n_rwkv, heads, HEAD_DIM, HEAD_DIM), F32)
    y_prompt, new_f, new_b = run_stream(x_prompt, (0, 1), zero_state, zero_state,
                                        x_prompt.shape[1], True)
    y_sample, _, _ = run_stream(x_sample, (1, 1 + n_dec), state_ctx_fwd, state_ctx_bwd,
                                GRID_W, False)
    return (y_prompt, y_sample, new_f, new_b)
```

```python
import functools
import math

import jax
import jax.numpy as jnp
from jax import lax
from jax.experimental import pallas as pl
from jax.experimental.pallas import tpu as pltpu

RMS_EPS = 1e-6
LN_EPS = 1e-5
GN_EPS = 64e-5
HEAD_DIM = 64
SGU_CHUNK = 128
GRID_W = 64

LANES = 128
SUBLANES = 8
VMEM_LIMIT = 56 * 1024 * 1024

TOKEN_TILE = 256
SCAN_BATCH = 4
SCAN_TCHUNK = 32
SCAN_ROWS = 16
SCAN_ACCS = 2
SCAN_SLOTS = 3
POST_TCHUNK = 16

BF16 = jnp.bfloat16
F32 = jnp.float32


def _params(*sem):
    return pltpu.CompilerParams(dimension_semantics=sem, vmem_limit_bytes=VMEM_LIMIT)


def _dot(a, b):
    return jnp.dot(a.astype(BF16), b, preferred_element_type=F32)


def _norm_mod(x, g, shift, scale):
    ms = jnp.mean(x * x, axis=-1, keepdims=True)
    return x * lax.rsqrt(ms + RMS_EPS) * g * (1.0 + scale) + shift


def _sigmoid(x):
    return 1.0 / (1.0 + jnp.exp(-x))


def _tree_sum(xs):
    while len(xs) > 1:
        xs = [xs[i] + xs[i + 1] for i in range(0, len(xs), 2)]
    return xs[0]


def _tile_major_shape(b, t, d):
    return (b, t // SUBLANES, d // LANES * SUBLANES, LANES)


def _put_tiles(ref, val):
    n = val.shape[0] // SUBLANES
    for p in range(val.shape[1] // LANES):
        ref[0, :, p * SUBLANES:(p + 1) * SUBLANES, :] = (
            val[:, p * LANES:(p + 1) * LANES].reshape(n, SUBLANES, LANES))


def _get_tiles(ref):
    _, n, rows, _ = ref.shape
    return jnp.concatenate(
        [ref[0, :, p * SUBLANES:(p + 1) * SUBLANES, :].reshape(n * SUBLANES, LANES)
         for p in range(rows // SUBLANES)], axis=1)


def _token_rows(u, pairs):
    return pl.ds(u, pairs, stride=SUBLANES)


def _mod_kernel(c_ref, w_ref, b_ref, o_ref):
    c = c_ref[...]
    sc = c * _sigmoid(c)
    o_ref[0] = jnp.dot(sc, w_ref[0], preferred_element_type=F32,
                       precision=lax.Precision.HIGHEST) + b_ref[0]


def _modulation(cond8, ada_w, ada_b):
    depth, d, n = ada_w.shape
    tn = n // 4
    return pl.pallas_call(
        _mod_kernel,
        grid=(depth, n // tn),
        in_specs=[
            pl.BlockSpec((SUBLANES, d), lambda l, j: (0, 0)),
            pl.BlockSpec((1, d, tn), lambda l, j: (l, 0, j)),
            pl.BlockSpec((1, 1, tn), lambda l, j: (l, 0, j)),
        ],
        out_specs=pl.BlockSpec((1, SUBLANES, tn), lambda l, j: (l, 0, j)),
        out_shape=jax.ShapeDtypeStruct((depth, SUBLANES, n), F32),
        compiler_params=_params("arbitrary", "arbitrary"),
        name="modulation",
    )(cond8, ada_w, ada_b.reshape(depth, 1, n))


def _mod_spec(mod):
    if mod.shape[0] == 1:
        return pl.BlockSpec((1, 1, mod.shape[2]), lambda b, i: (0, 0, 0))
    return pl.BlockSpec((1, 1, mod.shape[2]), lambda b, i: (b, 0, 0))


def _const_spec(a):
    nd = a.ndim
    return pl.BlockSpec(a.shape, lambda *_: (0,) * nd)


def _rw_proj_kernel(x_ref, xp_ref, xn_ref, mod_ref, nw_ref, mu_ref,
                    wr_ref, wk_ref, wv_ref, w1_ref, a1_ref, g1_ref,
                    w2_ref, a2_ref, w0_ref, a0_ref,
                    r_ref, k_ref, v_ref, df_ref, db_ref, af_ref, ab_ref, g_ref):
    i = pl.program_id(1)
    last = pl.num_programs(1) - 1
    d = x_ref.shape[2]
    tt = x_ref.shape[1]
    nw = nw_ref[...]
    shift = mod_ref[0, :, 0:d]
    scale = mod_ref[0, :, d:2 * d]
    h = _norm_mod(x_ref[0], nw, shift, scale)
    hp = _norm_mod(xp_ref[0], nw, shift, scale)[SUBLANES - 1:SUBLANES, :]
    hn = _norm_mod(xn_ref[0], nw, shift, scale)[0:1, :]
    hp = jnp.where(i == 0, 0.0, hp)
    hn = jnp.where(i == last, 0.0, hn)
    rows = lax.broadcasted_iota(jnp.int32, (tt, 1), 0)
    prev = jnp.where(rows == 0, hp, pltpu.roll(h, 1, axis=0))
    nxt = jnp.where(rows == tt - 1, hn, pltpu.roll(h, tt - 1, axis=0))
    dp = prev - h
    dn = nxt - h

    def shifted(idx):
        return h + dp * mu_ref[0, idx:idx + 1, :] + dn * mu_ref[1, idx:idx + 1, :]

    _put_tiles(r_ref, _dot(shifted(0), wr_ref[...]))
    _put_tiles(k_ref, _dot(shifted(2), wk_ref[...]))
    _put_tiles(v_ref, _dot(shifted(3), wv_ref[...]))
    g_ref[0] = _sigmoid(_dot(shifted(5), g1_ref[...]))
    lw = jnp.tanh(_dot(shifted(1), w1_ref[...]))
    la = _dot(shifted(4), a1_ref[...])
    lora = w2_ref.shape[1]
    decay_scale = math.exp(-0.5)
    for dirn, (d_ref, a_ref) in enumerate(((df_ref, af_ref), (db_ref, ab_ref))):
        w_raw = w0_ref[dirn:dirn + 1, :] + _dot(lw[:, dirn * lora:(dirn + 1) * lora], w2_ref[dirn])
        _put_tiles(d_ref, jnp.exp(-decay_scale * _sigmoid(w_raw)))
        a_raw = a0_ref[dirn:dirn + 1, :] + _dot(la[:, dirn * lora:(dirn + 1) * lora], a2_ref[dirn])
        _put_tiles(a_ref, _sigmoid(a_raw))


def _rw_proj(x, mod, nw, mu, wr, wk, wv, w1, a1, g1, w2, a2, w0, a0):
    b, t, d = x.shape
    tt = TOKEN_TILE
    nt = t // tt
    hb = tt // SUBLANES
    nhb = t // SUBLANES
    tok = pl.BlockSpec((1, tt, d), lambda bi, i: (bi, i, 0))
    prev = pl.BlockSpec((1, SUBLANES, d), lambda bi, i: (bi, jnp.maximum(i * hb - 1, 0), 0))
    nxt = pl.BlockSpec((1, SUBLANES, d), lambda bi, i: (bi, jnp.minimum((i + 1) * hb, nhb - 1), 0))
    consts = (nw, mu, wr, wk, wv, w1, a1, g1, w2, a2, w0, a0)
    gl = g1.shape[1]
    tiles = _tile_major_shape(b, t, d)
    tile_spec = pl.BlockSpec((1, tt // SUBLANES) + tiles[2:], lambda bi, i: (bi, i, 0, 0))
    out_shape = [jax.ShapeDtypeStruct(tiles, F32)] * 7 + [jax.ShapeDtypeStruct((b, t, gl), F32)]
    out_specs = [tile_spec] * 7 + [pl.BlockSpec((1, tt, gl), lambda bi, i: (bi, i, 0))]
    return pl.pallas_call(
        _rw_proj_kernel,
        grid=(b, nt),
        in_specs=[tok, prev, nxt, _mod_spec(mod)] + [_const_spec(c) for c in consts],
        out_specs=out_specs,
        out_shape=out_shape,
        compiler_params=_params("arbitrary", "arbitrary"),
        name="rw_proj",
    )(x, x, x, mod, *consts)


def _scan_kernel(rf_ref, rm_ref, kf_ref, km_ref, vf_ref, vm_ref, wf_ref, wm_ref, af_ref, am_ref,
                 s0_ref, kk_ref, ka_ref, rk_ref, y_ref, bv_ref, sfin_ref,
                 s_ref, ops_ref, sa_ref, stage_ref, cols_ref):
    c = pl.program_id(1)
    nb, ntb, pairs = rf_ref.shape[0], rf_ref.shape[1], rf_ref.shape[2] // SUBLANES
    tc = ntb * SUBLANES
    hd = s_ref.shape[0]

    @pl.when(c == 0)
    def _():
        s_ref[...] = s0_ref[0]

    operands = ((rf_ref, rm_ref), (kf_ref, km_ref), (vf_ref, vm_ref), (af_ref, am_ref),
                (wf_ref, wm_ref))

    def gather_rows(tb):
        half = tb % 2
        for op, (fwd_ref, bwd_ref) in enumerate(operands):
            for u in range(SUBLANES):
                for b in range(nb):
                    stage_ref[half, op, u, b * pairs:(b + 1) * pairs, :] = (
                        fwd_ref[b, tb, _token_rows(u, pairs), :])
                    stage_ref[half, op, u, (nb + b) * pairs:(nb + b + 1) * pairs, :] = (
                        bwd_ref[b, ntb - 1 - tb, _token_rows(SUBLANES - 1 - u, pairs), :])

    def transpose_rows(s):
        half = (s // SUBLANES) % 2
        u = s % SUBLANES
        for op in range(len(operands)):
            cols = stage_ref[half, op, u].T
            cols_ref[op] = jnp.concatenate([cols[0:hd], cols[hd:2 * hd]], axis=1)

    def prepare(s, slot):
        r = cols_ref[0]
        kraw = cols_ref[1]
        v = cols_ref[2]
        a = cols_ref[3]
        kkv = kraw * kk_ref[...]
        nrm = jnp.sqrt(jnp.sum(kkv * kkv, axis=0, keepdims=True))
        kk = kkv / jnp.maximum(nrm, 1e-12)
        kd = kraw * (1.0 + (a - 1.0) * ka_ref[...])
        ops_ref[slot, 0] = -kk
        ops_ref[slot, 1] = kk * a
        ops_ref[slot, 2] = kd
        ops_ref[slot, 3] = cols_ref[4]
        ops_ref[slot, 4] = r
        ops_ref[slot, 5] = v
        bv_ref[0, s] = v * jnp.sum(r * kd * rk_ref[...], axis=0, keepdims=True)

    def accumulate(acc, j, term):
        n = j % SCAN_ACCS
        acc[n] = term if acc[n] is None else acc[n] + term

    def first_sa(slot):
        for i0 in range(0, hd, SCAN_ROWS):
            rows = slice(i0, i0 + SCAN_ROWS)
            acc = [None] * SCAN_ACCS
            for j in range(hd):
                accumulate(acc, j, s_ref[j, rows, :] * ops_ref[slot, 0, j:j + 1, :])
            sa_ref[rows, :] = _tree_sum(acc)

    def advance(s, slot, nxt):
        for i0 in range(0, hd, SCAN_ROWS):
            rows = slice(i0, i0 + SCAN_ROWS)
            sa = sa_ref[rows, :]
            vr = ops_ref[slot, 5, rows, :]
            acc_y = [None] * SCAN_ACCS
            acc_a = [None] * SCAN_ACCS
            for j in range(hd):
                sn = (s_ref[j, rows, :] * ops_ref[slot, 3, j:j + 1, :]
                      + sa * ops_ref[slot, 1, j:j + 1, :] + vr * ops_ref[slot, 2, j:j + 1, :])
                s_ref[j, rows, :] = sn
                accumulate(acc_y, j, sn * ops_ref[slot, 4, j:j + 1, :])
                accumulate(acc_a, j, sn * ops_ref[nxt, 0, j:j + 1, :])
            y_ref[0, s, rows, :] = _tree_sum(acc_y)
            sa_ref[rows, :] = _tree_sum(acc_a)

    gather_rows(0)
    for s in range(2):
        transpose_rows(s)
        prepare(s, s)
    first_sa(0)

    def step(s, carry):
        ahead = jnp.minimum(s + 2, tc - 1)
        transpose_rows(ahead)
        advance(s, s % SCAN_SLOTS, (s + 1) % SCAN_SLOTS)
        prepare(ahead, (s + 2) % SCAN_SLOTS)
        return carry

    def token_block(tb, carry):
        gather_rows(jnp.minimum(tb + 1, ntb - 1))
        return lax.fori_loop(tb * SUBLANES, (tb + 1) * SUBLANES, step, carry)

    lax.fori_loop(0, ntb, token_block, 0)

    @pl.when(c == pl.num_programs(1) - 1)
    def _():
        sfin_ref[0] = s_ref[...]


def _scan(r, k, v, wf, wb, af, ab, s0, kk, ka, rk):
    b, tblocks, _, _ = r.shape
    t = tblocks * SUBLANES
    hd = HEAD_DIM
    g = b // SCAN_BATCH
    tc = SCAN_TCHUNK
    nt = t // tc
    blk = (SCAN_BATCH, tc // SUBLANES) + r.shape[2:]
    fwd = pl.BlockSpec(blk, lambda gi, c: (gi, c, 0, 0))
    bwd = pl.BlockSpec(blk, lambda gi, c: (gi, nt - 1 - c, 0, 0))
    step = pl.BlockSpec((1, tc, hd, LANES), lambda gi, c: (gi, c, 0, 0))
    state = pl.BlockSpec((1, hd, hd, LANES), lambda gi, c: (gi, 0, 0, 0))
    steps = jax.ShapeDtypeStruct((g, t, hd, LANES), F32)
    return pl.pallas_call(
        _scan_kernel,
        grid=(g, nt),
        in_specs=[fwd, bwd] * 5 + [state, _const_spec(kk), _const_spec(ka), _const_spec(rk)],
        out_specs=[step, step, state],
        out_shape=[steps, steps, jax.ShapeDtypeStruct((g, hd, hd, LANES), F32)],
        scratch_shapes=[pltpu.VMEM((hd, hd, LANES), F32),
                        pltpu.VMEM((SCAN_SLOTS, 6, hd, LANES), F32),
                        pltpu.VMEM((hd, LANES), F32),
                        pltpu.VMEM((2, 5, SUBLANES, 2 * SCAN_BATCH * r.shape[2] // SUBLANES, LANES), F32),
                        pltpu.VMEM((5, hd, LANES), F32)],
        compiler_params=_params("arbitrary", "arbitrary"),
        name="wkv_scan",
    )(r, r, k, k, v, v, wf, wb, af, ab, s0, kk, ka, rk)


def _rw_post_kernel(y_ref, ym_ref, bv_ref, bvm_ref, lw_ref, lb_ref, lo_ref, hi_ref):
    tp = y_ref.shape[1]
    hd = y_ref.shape[2]
    nb, ntb, pairs = lo_ref.shape[0], lo_ref.shape[1], lo_ref.shape[2] // SUBLANES
    quarter = LANES // 4
    lane = lax.broadcasted_iota(jnp.int32, (hd, LANES), 1)
    fwd_lane = (lane % (2 * quarter)) < quarter

    def swap_dirs(x):
        return jnp.where(fwd_lane, pltpu.roll(x, LANES - quarter, axis=1), pltpu.roll(x, quarter, axis=1))

    def token_block(tb, carry):
        for u in range(SUBLANES):
            t = tb * SUBLANES + u
            m = tp - 1 - t
            y = y_ref[0, t] + swap_dirs(ym_ref[0, m])
            bonus = bv_ref[0, t] + swap_dirs(bvm_ref[0, m])
            mean = jnp.mean(y, axis=0, keepdims=True)
            yc = y - mean
            var = jnp.mean(yc * yc, axis=0, keepdims=True)
            yn = yc * lax.rsqrt(var + GN_EPS) * lw_ref[...] + lb_ref[...]
            cols = (yn + bonus).T
            rows = jnp.concatenate([cols[0:hd], cols[hd:2 * hd]], axis=1)
            for b in range(nb):
                lo_ref[b, tb, _token_rows(u, pairs), :] = rows[b * pairs:(b + 1) * pairs]
                hi_ref[b, ntb - 1 - tb, _token_rows(SUBLANES - 1 - u, pairs), :] = (
                    rows[(nb + b) * pairs:(nb + b + 1) * pairs])
        return carry

    lax.fori_loop(0, ntb, token_block, 0)


def _rw_post(y, bv, lw, lb, d):
    g, t, hd, _ = y.shape
    tp = POST_TCHUNK
    nb = t // tp
    cur = pl.BlockSpec((1, tp, hd, LANES), lambda gi, c: (gi, c, 0, 0))
    mir = pl.BlockSpec((1, tp, hd, LANES), lambda gi, c: (gi, nb - 1 - c, 0, 0))
    tiles = _tile_major_shape(g * SCAN_BATCH, t // 2, d)
    blk = (SCAN_BATCH, tp // SUBLANES) + tiles[2:]
    lo = pl.BlockSpec(blk, lambda gi, c: (gi, c, 0, 0))
    hi = pl.BlockSpec(blk, lambda gi, c: (gi, nb // 2 - 1 - c, 0, 0))
    half_tokens = jax.ShapeDtypeStruct(tiles, F32)
    return pl.pallas_call(
        _rw_post_kernel,
        grid=(g, nb // 2),
        in_specs=[cur, mir, cur, mir, _const_spec(lw), _const_spec(lb)],
        out_specs=[lo, hi],
        out_shape=[half_tokens, half_tokens],
        compiler_params=_params("arbitrary", "arbitrary"),
        name="rw_post",
    )(y, y, bv, bv, lw, lb)


def _rw_out_kernel(x_ref, yn_ref, g_ref, mod_ref, g2_ref, wo_ref, o_ref):
    d = x_ref.shape[2]
    gate = mod_ref[0, :, 2 * d:3 * d]
    g = _dot(g_ref[0], g2_ref[...])
    out = _dot(_get_tiles(yn_ref) * g, wo_ref[...])
    o_ref[0] = x_ref[0] + gate * out


def _rw_out(x, yn, g, mod, g2, wo):
    b, t, d = x.shape
    tt = TOKEN_TILE
    tok = pl.BlockSpec((1, tt, d), lambda bi, i: (bi, i, 0))
    tiles = pl.BlockSpec((1, tt // SUBLANES) + yn.shape[2:], lambda bi, i: (bi, i, 0, 0))
    gspec = pl.BlockSpec((1, tt, g.shape[2]), lambda bi, i: (bi, i, 0))
    return pl.pallas_call(
        _rw_out_kernel,
        grid=(b, t // tt),
        in_specs=[tok, tiles, gspec, _mod_spec(mod), _const_spec(g2), _const_spec(wo)],
        out_specs=tok,
        out_shape=jax.ShapeDtypeStruct((b, t, d), F32),
        compiler_params=_params("arbitrary", "arbitrary"),
        name="rw_out",
    )(x, yn, g, mod, g2, wo)


def _ffn_up_kernel(x_ref, mod_ref, nw_ref, w_ref, o_ref):
    d = x_ref.shape[2]
    shift = mod_ref[0, :, 3 * d:4 * d]
    scale = mod_ref[0, :, 4 * d:5 * d]
    h = _norm_mod(x_ref[0], nw_ref[...], shift, scale)
    o_ref[0] = _dot(h, w_ref[...])


def _ffn_up(x, mod, nw, w_up):
    b, t, d = x.shape
    n = w_up.shape[1]
    tt = TOKEN_TILE
    ns = 2
    tn = n // ns
    if mod.shape[0] == 1:
        mspec = pl.BlockSpec((1, 1, mod.shape[2]), lambda j, bi, i: (0, 0, 0))
    else:
        mspec = pl.BlockSpec((1, 1, mod.shape[2]), lambda j, bi, i: (bi, 0, 0))
    return pl.pallas_call(
        _ffn_up_kernel,
        grid=(ns, b, t // tt),
        in_specs=[
            pl.BlockSpec((1, tt, d), lambda j, bi, i: (bi, i, 0)),
            mspec,
            pl.BlockSpec(nw.shape, lambda j, bi, i: (0, 0)),
            pl.BlockSpec((d, tn), lambda j, bi, i: (0, j)),
        ],
        out_specs=pl.BlockSpec((1, tt, tn), lambda j, bi, i: (bi, i, j)),
        out_shape=jax.ShapeDtypeStruct((b, t, n), F32),
        compiler_params=_params("arbitrary", "arbitrary", "arbitrary"),
        name="ffn_up",
    )(x, mod, nw, w_up)


def _conv_taps(cur, up, down, w_ref, width, vertical):
    n = cur.shape[0]
    col = lax.broadcasted_iota(jnp.int32, (n, 1), 0) % width
    if vertical:
        ext = jnp.concatenate([up, cur, down], axis=0)
        slabs = [(0, ext[0:n]), (1, cur), (2, ext[2 * width:2 * width + n])]
    else:
        slabs = [(1, cur)]
    z = []
    for dc in range(3):
        acc = None
        for dr, rows in slabs:
            term = rows * w_ref[3 * dr + dc:3 * dr + dc + 1, :]
            acc = term if acc is None else acc + term
        z.append(acc)
    left = jnp.where(col != 0, pltpu.roll(z[0], 1, axis=0), 0.0)
    right = jnp.where(col != width - 1, pltpu.roll(z[2], n - 1, axis=0), 0.0)
    return left + z[1] + right


def _ffn_down_kernel(*refs, width, vertical, final_norm):
    if vertical:
        (x_ref, mod_ref, val_ref, vup_ref, vdn_ref, gat_ref, gup_ref, gdn_ref,
         wcv_ref, wcg_ref, bv_ref, bg_ref, wd_ref, nf_ref, o_ref, acc_ref) = refs
    else:
        (x_ref, mod_ref, val_ref, gat_ref,
         wcv_ref, wcg_ref, bv_ref, bg_ref, wd_ref, nf_ref, o_ref, acc_ref) = refs
        vup_ref = vdn_ref = gup_ref = gdn_ref = None
    i = pl.program_id(1)
    c = pl.program_id(2)
    d = x_ref.shape[2]

    def halo(ref, edge):
        if ref is None:
            return None
        return jnp.where(edge, 0.0, ref[0])

    top = i == 0
    bottom = i == pl.num_programs(1) - 1
    val = _conv_taps(val_ref[0], halo(vup_ref, top), halo(vdn_ref, bottom), wcv_ref, width, vertical)
    gat = _conv_taps(gat_ref[0], halo(gup_ref, top), halo(gdn_ref, bottom), wcg_ref, width, vertical)
    val = val + bv_ref[...]
    gat = gat + bg_ref[...]
    act = gat * _sigmoid(gat) * val
    part = _dot(act, wd_ref[...])

    @pl.when(c == 0)
    def _():
        acc_ref[...] = part

    @pl.when(c != 0)
    def _():
        acc_ref[...] += part

    @pl.when(c == pl.num_programs(2) - 1)
    def _():
        gate = mod_ref[0, :, 5 * d:6 * d]
        y = x_ref[0] + gate * acc_ref[...]
        if final_norm:
            ms = jnp.mean(y * y, axis=-1, keepdims=True)
            y = y * lax.rsqrt(ms + RMS_EPS) * nf_ref[...]
        o_ref[0] = y


def _ffn_down(x, up, mod, w_conv, b_conv, w_down, nf, *, width, final_norm):
    b, t, d = x.shape
    f = w_down.shape[0]
    tt = TOKEN_TILE
    nt = t // tt
    vertical = width < t
    ck = f // 2
    nc = f // ck
    wc = w_conv.reshape(9, 2 * f)
    bc = b_conv.reshape(1, 2 * f)
    hb = tt // width if vertical else 1
    nhb = t // width if vertical else 1

    if mod.shape[0] == 1:
        mspec = pl.BlockSpec((1, 1, mod.shape[2]), lambda bi, i, c: (0, 0, 0))
    else:
        mspec = pl.BlockSpec((1, 1, mod.shape[2]), lambda bi, i, c: (bi, 0, 0))
    tok = pl.BlockSpec((1, tt, d), lambda bi, i, c: (bi, i, 0))

    def chan(off):
        main = pl.BlockSpec((1, tt, ck), lambda bi, i, c: (bi, i, c + off))
        if not vertical:
            return [main]
        upb = pl.BlockSpec((1, width, ck), lambda bi, i, c: (bi, jnp.maximum(i * hb - 1, 0), c + off))
        dnb = pl.BlockSpec((1, width, ck), lambda bi, i, c: (bi, jnp.minimum((i + 1) * hb, nhb - 1), c + off))
        return [main, upb, dnb]

    in_specs = ([tok, mspec] + chan(0) + chan(nc) + [
        pl.BlockSpec((9, ck), lambda bi, i, c: (0, c)),
        pl.BlockSpec((9, ck), lambda bi, i, c: (0, c + nc)),
        pl.BlockSpec((1, ck), lambda bi, i, c: (0, c)),
        pl.BlockSpec((1, ck), lambda bi, i, c: (0, c + nc)),
        pl.BlockSpec((ck, d), lambda bi, i, c: (c, 0)),
        pl.BlockSpec(nf.shape, lambda bi, i, c: (0, 0)),
    ])
    ups = [up] * (6 if vertical else 2)
    kern = functools.partial(_ffn_down_kernel, width=width, vertical=vertical, final_norm=final_norm)
    return pl.pallas_call(
        kern,
        grid=(b, nt, nc),
        in_specs=in_specs,
        out_specs=tok,
        out_shape=jax.ShapeDtypeStruct((b, t, d), F32),
        scratch_shapes=[pltpu.VMEM((tt, d), F32)],
        compiler_params=_params("arbitrary", "arbitrary", "arbitrary"),
        name="ffn_down",
    )(x, mod, *ups, wc, wc, bc, bc, w_down, nf)


def _sgu_kernel(x_ref, mod_ref, nw_ref, win_ref, lnw_ref, lnb_ref, ws_ref, bs_ref, wout_ref, o_ref):
    d = x_ref.shape[2]
    tt = x_ref.shape[1]
    e = wout_ref.shape[0]
    groups = ws_ref.shape[0]
    gw = e // groups
    shift = mod_ref[0, :, 0:d]
    scale = mod_ref[0, :, d:2 * d]
    gate = mod_ref[0, :, 2 * d:3 * d]
    x = x_ref[0]
    h = _norm_mod(x, nw_ref[...], shift, scale)
    z = _dot(h, win_ref[...])
    z = 0.5 * z * (1.0 + lax.erf(z * (1.0 / math.sqrt(2.0))))
    u = z[:, :e]
    v = z[:, e:]
    mu = jnp.mean(v, axis=-1, keepdims=True)
    vc = v - mu
    var = jnp.mean(vc * vc, axis=-1, keepdims=True)
    vn = (vc * lax.rsqrt(var + LN_EPS) * lnw_ref[...] + lnb_ref[...]).astype(BF16)
    chunks = []
    for p in range(tt // SGU_CHUNK):
        rows = slice(p * SGU_CHUNK, (p + 1) * SGU_CHUNK)
        parts = [jnp.dot(ws_ref[gi], vn[rows, gi * gw:(gi + 1) * gw], preferred_element_type=F32)
                 for gi in range(groups)]
        vm = jnp.concatenate(parts, axis=1) + bs_ref[...]
        chunks.append(u[rows] * vm)
    gated = jnp.concatenate(chunks, axis=0)
    out = _dot(gated, wout_ref[...])
    o_ref[0] = x + gate * out


def _sgu(x, mod, nw, w_in, ln_w, ln_b, w_s, b_full, w_out):
    b, t, d = x.shape
    tt = TOKEN_TILE
    tok = pl.BlockSpec((1, tt, d), lambda bi, i: (bi, i, 0))
    consts = (nw, w_in, ln_w, ln_b, w_s, b_full, w_out)
    return pl.pallas_call(
        _sgu_kernel,
        grid=(b, t // tt),
        in_specs=[tok, _mod_spec(mod)] + [_const_spec(c) for c in consts],
        out_specs=tok,
        out_shape=jax.ShapeDtypeStruct((b, t, d), F32),
        compiler_params=_params("arbitrary", "arbitrary"),
        name="sgu",
    )(x, mod, *consts)


def _lane_param(p, heads):
    m = p.reshape(heads // 2, 2, HEAD_DIM).transpose(2, 1, 0)
    m = jnp.broadcast_to(m[:, :, None, None, :], (HEAD_DIM, 2, 2, SCAN_BATCH, heads // 2))
    return m.reshape(HEAD_DIM, LANES)


def _state_to_scan(sf, sb):
    b, h = sf.shape[:2]
    g = b // SCAN_BATCH
    s = jnp.stack([sf, sb], axis=0).reshape(2, g, SCAN_BATCH, h // 2, 2, HEAD_DIM, HEAD_DIM)
    return s.transpose(1, 6, 5, 4, 0, 2, 3).reshape(g, HEAD_DIM, HEAD_DIM, LANES)


def _state_from_scan(s, heads):
    g = s.shape[0]
    s = s.reshape(g, HEAD_DIM, HEAD_DIM, 2, 2, SCAN_BATCH, heads // 2)
    s = s.transpose(4, 0, 5, 6, 3, 2, 1).reshape(2, g * SCAN_BATCH, heads, HEAD_DIM, HEAD_DIM)
    return s[0], s[1]


def kernel(x_prompt, x_sample, state_ctx_fwd, state_ctx_bwd, c, c_ctx, ada_w, ada_b, norm_mix, norm_ffn, ffn_up, ffn_conv, ffn_conv_b, ffn_down, norm_final, rw_mu, rw_wr, rw_wk, rw_wv, rw_wo, rw_w0, rw_w1, rw_w2, rw_a0, rw_a1, rw_a2, rw_g1, rw_g2, rw_kk, rw_ka, rw_rk, rw_lnx_w, rw_lnx_b, sg_in, sg_ln_w, sg_ln_b, sg_ws, sg_bs, sg_out):
    depth, d, _ = ada_w.shape
    heads = d // HEAD_DIM
    n_dec = c.shape[0]
    assert SCAN_BATCH * heads * 2 == LANES
    assert n_dec + 1 <= SUBLANES

    cond8 = jnp.concatenate(
        [c_ctx[None, :], c, jnp.zeros((SUBLANES - 1 - n_dec, d), F32)], axis=0)
    mod_all = _modulation(cond8, ada_w, ada_b)
    nf = norm_final.reshape(1, d)

    def run_stream(x, mod_rows, s0f, s0b, width, want_state):
        b, t, _ = x.shape
        new_f, new_b = [], []
        for i in range(depth):
            mod = mod_all[i, mod_rows[0]:mod_rows[1]][:, None, :]
            j = i // 2
            nw_mix = norm_mix[i].reshape(1, d)
            if i % 2 == 0:
                w1 = jnp.concatenate([rw_w1[j, 0], rw_w1[j, 1]], axis=1).astype(BF16)
                a1 = jnp.concatenate([rw_a1[j, 0], rw_a1[j, 1]], axis=1).astype(BF16)
                r, k, v, df, db, af, ab, g = _rw_proj(
                    x, mod, nw_mix, rw_mu[j],
                    rw_wr[j].astype(BF16), rw_wk[j].astype(BF16), rw_wv[j].astype(BF16),
                    w1, a1, rw_g1[j].astype(BF16),
                    rw_w2[j].astype(BF16), rw_a2[j].astype(BF16), rw_w0[j], rw_a0[j])
                s0 = _state_to_scan(s0f[:, j], s0b[:, j])
                ys, bvs, sfin = _scan(
                    r, k, v, df, db, af, ab, s0,
                    _lane_param(rw_kk[j], heads), _lane_param(rw_ka[j], heads),
                    _lane_param(rw_rk[j].reshape(-1), heads))
                lo, hi = _rw_post(ys, bvs, _lane_param(rw_lnx_w[j], heads),
                                  _lane_param(rw_lnx_b[j], heads), d)
                yn = jnp.concatenate([lo, hi], axis=1)
                x = _rw_out(x, yn, g, mod, rw_g2[j].astype(BF16), rw_wo[j].astype(BF16))
                if want_state:
                    sf, sb = _state_from_scan(sfin, heads)
                    new_f.append(sf)
                    new_b.append(sb)
            else:
                e = sg_out.shape[1]
                b_full = jnp.repeat(sg_bs[j].T, e // sg_bs.shape[1], axis=1)
                x = _sgu(x, mod, nw_mix, sg_in[j].astype(BF16),
                         sg_ln_w[j].reshape(1, e), sg_ln_b[j].reshape(1, e),
                         sg_ws[j].astype(BF16), b_full, sg_out[j].astype(BF16))
            up = _ffn_up(x, mod, norm_ffn[i].reshape(1, d), ffn_up[i].astype(BF16))
            x = _ffn_down(x, up, mod, ffn_conv[i], ffn_conv_b[i], ffn_down[i].astype(BF16), nf,
                          width=width, final_norm=(i == depth - 1))
        if want_state:
            return x, jnp.stack(new_f, axis=1), jnp.stack(new_b, axis=1)
        return x, None, None

    bp = x_prompt.shape[0]
    n_rwkv = state_ctx_fwd.shape[1]
    zero_state = jnp.zeros((bp, n_rwkv, heads, HEAD_DIM, HEAD_DIM), F32)
    y_prompt, new_f, new_b = run_stream(x_prompt, (0, 1), zero_state, zero_state,
                                        x_prompt.shape[1], True)
    y_sample, _, _ = run_stream(x_sample, (1, 1 + n_dec), state_ctx_fwd, state_ctx_bwd,
                                GRID_W, False)
    return (y_prompt, y_sample, new_f, new_b)
```

```python
import functools
import math

import jax
import jax.numpy as jnp
from jax import lax
from jax.experimental import pallas as pl
from jax.experimental.pallas import tpu as pltpu

RMS_EPS = 1e-6
LN_EPS = 1e-5
GN_EPS = 64e-5
HEAD_DIM = 64
SGU_CHUNK = 128
GRID_W = 64

LANES = 128
SUBLANES = 8
VMEM_LIMIT = 56 * 1024 * 1024

TOKEN_TILE = 256
FFN_TILE = 512
FFN_CHANNEL_BLOCKS = 2
SCAN_BATCH = 4
SCAN_TCHUNK = 32
SCAN_ROWS = 64
SCAN_ACCS = 1
SCAN_SLOTS = 3
POST_TCHUNK = 16

BF16 = jnp.bfloat16
F32 = jnp.float32


def _params(*sem):
    return pltpu.CompilerParams(dimension_semantics=sem, vmem_limit_bytes=VMEM_LIMIT)


def _dot(a, b):
    return jnp.dot(a.astype(BF16), b, preferred_element_type=F32)


def _norm_mod(x, g, shift, scale):
    ms = jnp.mean(x * x, axis=-1, keepdims=True)
    return x * lax.rsqrt(ms + RMS_EPS) * g * (1.0 + scale) + shift


def _sigmoid(x):
    return 1.0 / (1.0 + jnp.exp(-x))


def _tree_sum(xs):
    while len(xs) > 1:
        xs = [xs[i] + xs[i + 1] for i in range(0, len(xs), 2)]
    return xs[0]


def _tile_major_shape(b, t, d):
    return (b, t // SUBLANES, d // LANES * SUBLANES, LANES)


def _put_tiles(ref, val):
    n = val.shape[0] // SUBLANES
    for p in range(val.shape[1] // LANES):
        ref[0, :, p * SUBLANES:(p + 1) * SUBLANES, :] = (
            val[:, p * LANES:(p + 1) * LANES].reshape(n, SUBLANES, LANES))


def _get_tiles(ref):
    _, n, rows, _ = ref.shape
    return jnp.concatenate(
        [ref[0, :, p * SUBLANES:(p + 1) * SUBLANES, :].reshape(n * SUBLANES, LANES)
         for p in range(rows // SUBLANES)], axis=1)


def _token_rows(u, pairs):
    return pl.ds(u, pairs, stride=SUBLANES)


def _mod_kernel(c_ref, w_ref, b_ref, o_ref):
    c = c_ref[...]
    sc = c * _sigmoid(c)
    o_ref[0] = jnp.dot(sc, w_ref[0], preferred_element_type=F32,
                       precision=lax.Precision.HIGHEST) + b_ref[0]


def _modulation(cond8, ada_w, ada_b):
    depth, d, n = ada_w.shape
    tn = n // 4
    return pl.pallas_call(
        _mod_kernel,
        grid=(depth, n // tn),
        in_specs=[
            pl.BlockSpec((SUBLANES, d), lambda l, j: (0, 0)),
            pl.BlockSpec((1, d, tn), lambda l, j: (l, 0, j)),
            pl.BlockSpec((1, 1, tn), lambda l, j: (l, 0, j)),
        ],
        out_specs=pl.BlockSpec((1, SUBLANES, tn), lambda l, j: (l, 0, j)),
        out_shape=jax.ShapeDtypeStruct((depth, SUBLANES, n), F32),
        compiler_params=_params("arbitrary", "arbitrary"),
        name="modulation",
    )(cond8, ada_w, ada_b.reshape(depth, 1, n))


def _mod_spec(mod):
    if mod.shape[0] == 1:
        return pl.BlockSpec((1, 1, mod.shape[2]), lambda b, i: (0, 0, 0))
    return pl.BlockSpec((1, 1, mod.shape[2]), lambda b, i: (b, 0, 0))


def _const_spec(a):
    nd = a.ndim
    return pl.BlockSpec(a.shape, lambda *_: (0,) * nd)


def _rw_proj_kernel(x_ref, xp_ref, xn_ref, mod_ref, nw_ref, mu_ref,
                    wr_ref, wk_ref, wv_ref, w1_ref, a1_ref, g1_ref,
                    w2_ref, a2_ref, w0_ref, a0_ref,
                    r_ref, k_ref, v_ref, df_ref, db_ref, af_ref, ab_ref, g_ref):
    i = pl.program_id(1)
    last = pl.num_programs(1) - 1
    d = x_ref.shape[2]
    tt = x_ref.shape[1]
    nw = nw_ref[...]
    shift = mod_ref[0, :, 0:d]
    scale = mod_ref[0, :, d:2 * d]
    h = _norm_mod(x_ref[0], nw, shift, scale)
    hp = _norm_mod(xp_ref[0], nw, shift, scale)[SUBLANES - 1:SUBLANES, :]
    hn = _norm_mod(xn_ref[0], nw, shift, scale)[0:1, :]
    hp = jnp.where(i == 0, 0.0, hp)
    hn = jnp.where(i == last, 0.0, hn)
    rows = lax.broadcasted_iota(jnp.int32, (tt, 1), 0)
    prev = jnp.where(rows == 0, hp, pltpu.roll(h, 1, axis=0))
    nxt = jnp.where(rows == tt - 1, hn, pltpu.roll(h, tt - 1, axis=0))
    dp = prev - h
    dn = nxt - h

    def shifted(idx):
        return h + dp * mu_ref[0, idx:idx + 1, :] + dn * mu_ref[1, idx:idx + 1, :]

    _put_tiles(r_ref, _dot(shifted(0), wr_ref[...]))
    _put_tiles(k_ref, _dot(shifted(2), wk_ref[...]))
    _put_tiles(v_ref, _dot(shifted(3), wv_ref[...]))
    g_ref[0] = _sigmoid(_dot(shifted(5), g1_ref[...]))
    lw = jnp.tanh(_dot(shifted(1), w1_ref[...]))
    la = _dot(shifted(4), a1_ref[...])
    lora = w2_ref.shape[1]
    decay_scale = math.exp(-0.5)
    for dirn, (d_ref, a_ref) in enumerate(((df_ref, af_ref), (db_ref, ab_ref))):
        w_raw = w0_ref[dirn:dirn + 1, :] + _dot(lw[:, dirn * lora:(dirn + 1) * lora], w2_ref[dirn])
        _put_tiles(d_ref, jnp.exp(-decay_scale * _sigmoid(w_raw)))
        a_raw = a0_ref[dirn:dirn + 1, :] + _dot(la[:, dirn * lora:(dirn + 1) * lora], a2_ref[dirn])
        _put_tiles(a_ref, _sigmoid(a_raw))


def _rw_proj(x, mod, nw, mu, wr, wk, wv, w1, a1, g1, w2, a2, w0, a0):
    b, t, d = x.shape
    tt = TOKEN_TILE
    nt = t // tt
    hb = tt // SUBLANES
    nhb = t // SUBLANES
    tok = pl.BlockSpec((1, tt, d), lambda bi, i: (bi, i, 0))
    prev = pl.BlockSpec((1, SUBLANES, d), lambda bi, i: (bi, jnp.maximum(i * hb - 1, 0), 0))
    nxt = pl.BlockSpec((1, SUBLANES, d), lambda bi, i: (bi, jnp.minimum((i + 1) * hb, nhb - 1), 0))
    consts = (nw, mu, wr, wk, wv, w1, a1, g1, w2, a2, w0, a0)
    gl = g1.shape[1]
    tiles = _tile_major_shape(b, t, d)
    tile_spec = pl.BlockSpec((1, tt // SUBLANES) + tiles[2:], lambda bi, i: (bi, i, 0, 0))
    out_shape = [jax.ShapeDtypeStruct(tiles, F32)] * 7 + [jax.ShapeDtypeStruct((b, t, gl), F32)]
    out_specs = [tile_spec] * 7 + [pl.BlockSpec((1, tt, gl), lambda bi, i: (bi, i, 0))]
    return pl.pallas_call(
        _rw_proj_kernel,
        grid=(b, nt),
        in_specs=[tok, prev, nxt, _mod_spec(mod)] + [_const_spec(c) for c in consts],
        out_specs=out_specs,
        out_shape=out_shape,
        compiler_params=_params("arbitrary", "arbitrary"),
        name="rw_proj",
    )(x, x, x, mod, *consts)


def _scan_kernel(rf_ref, rm_ref, kf_ref, km_ref, vf_ref, vm_ref, wf_ref, wm_ref, af_ref, am_ref,
                 s0_ref, kk_ref, ka_ref, rk_ref, y_ref, bv_ref, sfin_ref,
                 s_ref, ops_ref, sa_ref, stage_ref, cols_ref):
    c = pl.program_id(1)
    nb, ntb, pairs = rf_ref.shape[0], rf_ref.shape[1], rf_ref.shape[2] // SUBLANES
    tc = ntb * SUBLANES
    hd = s_ref.shape[0]

    @pl.when(c == 0)
    def _():
        s_ref[...] = s0_ref[0]

    operands = ((rf_ref, rm_ref), (kf_ref, km_ref), (vf_ref, vm_ref), (af_ref, am_ref),
                (wf_ref, wm_ref))

    def gather_rows(tb):
        half = tb % 2
        for op, (fwd_ref, bwd_ref) in enumerate(operands):
            for u in range(SUBLANES):
                for b in range(nb):
                    stage_ref[half, op, u, b * pairs:(b + 1) * pairs, :] = (
                        fwd_ref[b, tb, _token_rows(u, pairs), :])
                    stage_ref[half, op, u, (nb + b) * pairs:(nb + b + 1) * pairs, :] = (
                        bwd_ref[b, ntb - 1 - tb, _token_rows(SUBLANES - 1 - u, pairs), :])

    def transpose_rows(s):
        half = (s // SUBLANES) % 2
        u = s % SUBLANES
        for op in range(len(operands)):
            cols = stage_ref[half, op, u].T
            cols_ref[op] = jnp.concatenate([cols[0:hd], cols[hd:2 * hd]], axis=1)

    def prepare(s, slot):
        r = cols_ref[0]
        kraw = cols_ref[1]
        v = cols_ref[2]
        a = cols_ref[3]
        kkv = kraw * kk_ref[...]
        nrm = jnp.sqrt(jnp.sum(kkv * kkv, axis=0, keepdims=True))
        kk = kkv / jnp.maximum(nrm, 1e-12)
        kd = kraw * (1.0 + (a - 1.0) * ka_ref[...])
        ops_ref[slot, 0] = -kk
        ops_ref[slot, 1] = kk * a
        ops_ref[slot, 2] = kd
        ops_ref[slot, 3] = cols_ref[4]
        ops_ref[slot, 4] = r
        ops_ref[slot, 5] = v
        bv_ref[0, s] = v * jnp.sum(r * kd * rk_ref[...], axis=0, keepdims=True)

    def accumulate(acc, j, term):
        n = j % SCAN_ACCS
        acc[n] = term if acc[n] is None else acc[n] + term

    def first_sa(slot):
        for i0 in range(0, hd, SCAN_ROWS):
            rows = slice(i0, i0 + SCAN_ROWS)
            acc = [None] * SCAN_ACCS
            for j in range(hd):
                accumulate(acc, j, s_ref[j, rows, :] * ops_ref[slot, 0, j:j + 1, :])
            sa_ref[rows, :] = _tree_sum(acc)

    def advance(s, slot, nxt):
        for i0 in range(0, hd, SCAN_ROWS):
            rows = slice(i0, i0 + SCAN_ROWS)
            sa = sa_ref[rows, :]
            vr = ops_ref[slot, 5, rows, :]
            acc_y = [None] * SCAN_ACCS
            acc_a = [None] * SCAN_ACCS
            for j in range(hd):
                sn = (s_ref[j, rows, :] * ops_ref[slot, 3, j:j + 1, :]
                      + sa * ops_ref[slot, 1, j:j + 1, :] + vr * ops_ref[slot, 2, j:j + 1, :])
                s_ref[j, rows, :] = sn
                accumulate(acc_y, j, sn * ops_ref[slot, 4, j:j + 1, :])
                accumulate(acc_a, j, sn * ops_ref[nxt, 0, j:j + 1, :])
            y_ref[0, s, rows, :] = _tree_sum(acc_y)
            sa_ref[rows, :] = _tree_sum(acc_a)

    gather_rows(0)
    for s in range(2):
        transpose_rows(s)
        prepare(s, s)
    first_sa(0)

    def step(s, carry):
        ahead = jnp.minimum(s + 2, tc - 1)
        transpose_rows(ahead)
        advance(s, s % SCAN_SLOTS, (s + 1) % SCAN_SLOTS)
        prepare(ahead, (s + 2) % SCAN_SLOTS)
        return carry

    def token_block(tb, carry):
        gather_rows(jnp.minimum(tb + 1, ntb - 1))
        return lax.fori_loop(tb * SUBLANES, (tb + 1) * SUBLANES, step, carry)

    lax.fori_loop(0, ntb, token_block, 0)

    @pl.when(c == pl.num_programs(1) - 1)
    def _():
        sfin_ref[0] = s_ref[...]


def _scan(r, k, v, wf, wb, af, ab, s0, kk, ka, rk):
    b, tblocks, _, _ = r.shape
    t = tblocks * SUBLANES
    hd = HEAD_DIM
    g = b // SCAN_BATCH
    tc = SCAN_TCHUNK
    nt = t // tc
    blk = (SCAN_BATCH, tc // SUBLANES) + r.shape[2:]
    fwd = pl.BlockSpec(blk, lambda gi, c: (gi, c, 0, 0))
    bwd = pl.BlockSpec(blk, lambda gi, c: (gi, nt - 1 - c, 0, 0))
    step = pl.BlockSpec((1, tc, hd, LANES), lambda gi, c: (gi, c, 0, 0))
    state = pl.BlockSpec((1, hd, hd, LANES), lambda gi, c: (gi, 0, 0, 0))
    steps = jax.ShapeDtypeStruct((g, t, hd, LANES), F32)
    return pl.pallas_call(
        _scan_kernel,
        grid=(g, nt),
        in_specs=[fwd, bwd] * 5 + [state, _const_spec(kk), _const_spec(ka), _const_spec(rk)],
        out_specs=[step, step, state],
        out_shape=[steps, steps, jax.ShapeDtypeStruct((g, hd, hd, LANES), F32)],
        scratch_shapes=[pltpu.VMEM((hd, hd, LANES), F32),
                        pltpu.VMEM((SCAN_SLOTS, 6, hd, LANES), F32),
                        pltpu.VMEM((hd, LANES), F32),
                        pltpu.VMEM((2, 5, SUBLANES, 2 * SCAN_BATCH * r.shape[2] // SUBLANES, LANES), F32),
                        pltpu.VMEM((5, hd, LANES), F32)],
        compiler_params=_params("arbitrary", "arbitrary"),
        name="wkv_scan",
    )(r, r, k, k, v, v, wf, wb, af, ab, s0, kk, ka, rk)


def _rw_post_kernel(y_ref, ym_ref, bv_ref, bvm_ref, lw_ref, lb_ref, lo_ref, hi_ref):
    tp = y_ref.shape[1]
    hd = y_ref.shape[2]
    nb, ntb, pairs = lo_ref.shape[0], lo_ref.shape[1], lo_ref.shape[2] // SUBLANES
    quarter = LANES // 4
    lane = lax.broadcasted_iota(jnp.int32, (hd, LANES), 1)
    fwd_lane = (lane % (2 * quarter)) < quarter

    def swap_dirs(x):
        return jnp.where(fwd_lane, pltpu.roll(x, LANES - quarter, axis=1), pltpu.roll(x, quarter, axis=1))

    def token_block(tb, carry):
        for u in range(SUBLANES):
            t = tb * SUBLANES + u
            m = tp - 1 - t
            y = y_ref[0, t] + swap_dirs(ym_ref[0, m])
            bonus = bv_ref[0, t] + swap_dirs(bvm_ref[0, m])
            mean = jnp.mean(y, axis=0, keepdims=True)
            yc = y - mean
            var = jnp.mean(yc * yc, axis=0, keepdims=True)
            yn = yc * lax.rsqrt(var + GN_EPS) * lw_ref[...] + lb_ref[...]
            cols = (yn + bonus).T
            rows = jnp.concatenate([cols[0:hd], cols[hd:2 * hd]], axis=1)
            for b in range(nb):
                lo_ref[b, tb, _token_rows(u, pairs), :] = rows[b * pairs:(b + 1) * pairs]
                hi_ref[b, ntb - 1 - tb, _token_rows(SUBLANES - 1 - u, pairs), :] = (
                    rows[(nb + b) * pairs:(nb + b + 1) * pairs])
        return carry

    lax.fori_loop(0, ntb, token_block, 0)


def _rw_post(y, bv, lw, lb, d):
    g, t, hd, _ = y.shape
    tp = POST_TCHUNK
    nb = t // tp
    cur = pl.BlockSpec((1, tp, hd, LANES), lambda gi, c: (gi, c, 0, 0))
    mir = pl.BlockSpec((1, tp, hd, LANES), lambda gi, c: (gi, nb - 1 - c, 0, 0))
    tiles = _tile_major_shape(g * SCAN_BATCH, t // 2, d)
    blk = (SCAN_BATCH, tp // SUBLANES) + tiles[2:]
    lo = pl.BlockSpec(blk, lambda gi, c: (gi, c, 0, 0))
    hi = pl.BlockSpec(blk, lambda gi, c: (gi, nb // 2 - 1 - c, 0, 0))
    half_tokens = jax.ShapeDtypeStruct(tiles, F32)
    return pl.pallas_call(
        _rw_post_kernel,
        grid=(g, nb // 2),
        in_specs=[cur, mir, cur, mir, _const_spec(lw), _const_spec(lb)],
        out_specs=[lo, hi],
        out_shape=[half_tokens, half_tokens],
        compiler_params=_params("arbitrary", "arbitrary"),
        name="rw_post",
    )(y, y, bv, bv, lw, lb)


def _rw_out_kernel(x_ref, yn_ref, g_ref, mod_ref, g2_ref, wo_ref, o_ref):
    d = x_ref.shape[2]
    gate = mod_ref[0, :, 2 * d:3 * d]
    g = _dot(g_ref[0], g2_ref[...])
    out = _dot(_get_tiles(yn_ref) * g, wo_ref[...])
    o_ref[0] = x_ref[0] + gate * out


def _rw_out(x, yn, g, mod, g2, wo):
    b, t, d = x.shape
    tt = TOKEN_TILE
    tok = pl.BlockSpec((1, tt, d), lambda bi, i: (bi, i, 0))
    tiles = pl.BlockSpec((1, tt // SUBLANES) + yn.shape[2:], lambda bi, i: (bi, i, 0, 0))
    gspec = pl.BlockSpec((1, tt, g.shape[2]), lambda bi, i: (bi, i, 0))
    return pl.pallas_call(
        _rw_out_kernel,
        grid=(b, t // tt),
        in_specs=[tok, tiles, gspec, _mod_spec(mod), _const_spec(g2), _const_spec(wo)],
        out_specs=tok,
        out_shape=jax.ShapeDtypeStruct((b, t, d), F32),
        compiler_params=_params("arbitrary", "arbitrary"),
        name="rw_out",
    )(x, yn, g, mod, g2, wo)


def _conv_taps(cur, up, down, w_ref, width, vertical):
    n = cur.shape[0]
    col = lax.broadcasted_iota(jnp.int32, (n, 1), 0) % width
    if vertical:
        ext = jnp.concatenate([up, cur, down], axis=0)
        slabs = [(0, ext[0:n]), (1, cur), (2, ext[2 * width:2 * width + n])]
    else:
        slabs = [(1, cur)]
    z = []
    for dc in range(3):
        acc = None
        for dr, rows in slabs:
            term = rows * w_ref[3 * dr + dc:3 * dr + dc + 1, :]
            acc = term if acc is None else acc + term
        z.append(acc)
    left = jnp.where(col != 0, pltpu.roll(z[0], 1, axis=0), 0.0)
    right = jnp.where(col != width - 1, pltpu.roll(z[2], n - 1, axis=0), 0.0)
    return left + z[1] + right


def _ffn_kernel(*refs, width, vertical, final_norm):
    if vertical:
        (x_ref, xup_ref, xdn_ref, mod_ref, nw_ref, wuv_ref, wug_ref,
         wcv_ref, wcg_ref, bv_ref, bg_ref, wd_ref, nf_ref, o_ref, h_ref, acc_ref) = refs
    else:
        (x_ref, mod_ref, nw_ref, wuv_ref, wug_ref,
         wcv_ref, wcg_ref, bv_ref, bg_ref, wd_ref, nf_ref, o_ref, h_ref, acc_ref) = refs
    i = pl.program_id(1)
    c = pl.program_id(2)
    tt, d = x_ref.shape[1], x_ref.shape[2]
    pad = width if vertical else 0

    @pl.when(c == 0)
    def _():
        shift = mod_ref[0, :, 3 * d:4 * d]
        scale = mod_ref[0, :, 4 * d:5 * d]
        h_ref[pad:pad + tt, :] = _norm_mod(x_ref[0], nw_ref[...], shift, scale).astype(BF16)
        if vertical:
            h_ref[0:pad, :] = _norm_mod(xup_ref[0], nw_ref[...], shift, scale).astype(BF16)
            h_ref[pad + tt:, :] = _norm_mod(xdn_ref[0], nw_ref[...], shift, scale).astype(BF16)

    top = i == 0
    bottom = i == pl.num_programs(1) - 1

    def conv(w_up_ref, w_conv_ref):
        up = jnp.dot(h_ref[...], w_up_ref[...], preferred_element_type=F32)
        if not vertical:
            return _conv_taps(up, None, None, w_conv_ref, width, vertical)
        above = jnp.where(top, 0.0, up[0:pad])
        below = jnp.where(bottom, 0.0, up[pad + tt:])
        return _conv_taps(up[pad:pad + tt], above, below, w_conv_ref, width, vertical)

    val = conv(wuv_ref, wcv_ref) + bv_ref[...]
    gat = conv(wug_ref, wcg_ref) + bg_ref[...]
    act = gat * _sigmoid(gat) * val
    part = _dot(act, wd_ref[...])

    @pl.when(c == 0)
    def _():
        acc_ref[...] = part

    @pl.when(c != 0)
    def _():
        acc_ref[...] += part

    @pl.when(c == pl.num_programs(2) - 1)
    def _():
        gate = mod_ref[0, :, 5 * d:6 * d]
        y = x_ref[0] + gate * acc_ref[...]
        if final_norm:
            ms = jnp.mean(y * y, axis=-1, keepdims=True)
            y = y * lax.rsqrt(ms + RMS_EPS) * nf_ref[...]
        o_ref[0] = y


def _ffn(x, mod, nw, w_up, w_conv, b_conv, w_down, nf, *, width, final_norm):
    b, t, d = x.shape
    f = w_down.shape[0]
    tt = min(FFN_TILE, t)
    nt = t // tt
    vertical = width < t
    ck = f // FFN_CHANNEL_BLOCKS
    nc = FFN_CHANNEL_BLOCKS
    wc = w_conv.reshape(9, 2 * f)
    bc = b_conv.reshape(1, 2 * f)
    hb = tt // width if vertical else 1
    nhb = t // width if vertical else 1

    if mod.shape[0] == 1:
        mspec = pl.BlockSpec((1, 1, mod.shape[2]), lambda bi, i, c: (0, 0, 0))
    else:
        mspec = pl.BlockSpec((1, 1, mod.shape[2]), lambda bi, i, c: (bi, 0, 0))
    tok = pl.BlockSpec((1, tt, d), lambda bi, i, c: (bi, i, 0))
    xs = [x]
    x_specs = [tok]
    if vertical:
        xs += [x, x]
        x_specs += [
            pl.BlockSpec((1, width, d), lambda bi, i, c: (bi, jnp.maximum(i * hb - 1, 0), 0)),
            pl.BlockSpec((1, width, d), lambda bi, i, c: (bi, jnp.minimum((i + 1) * hb, nhb - 1), 0)),
        ]
    in_specs = x_specs + [
        mspec,
        pl.BlockSpec(nw.shape, lambda bi, i, c: (0, 0)),
        pl.BlockSpec((d, ck), lambda bi, i, c: (0, c)),
        pl.BlockSpec((d, ck), lambda bi, i, c: (0, c + nc)),
        pl.BlockSpec((9, ck), lambda bi, i, c: (0, c)),
        pl.BlockSpec((9, ck), lambda bi, i, c: (0, c + nc)),
        pl.BlockSpec((1, ck), lambda bi, i, c: (0, c)),
        pl.BlockSpec((1, ck), lambda bi, i, c: (0, c + nc)),
        pl.BlockSpec((ck, d), lambda bi, i, c: (c, 0)),
        pl.BlockSpec(nf.shape, lambda bi, i, c: (0, 0)),
    ]
    rows = tt + 2 * width if vertical else tt
    kern = functools.partial(_ffn_kernel, width=width, vertical=vertical, final_norm=final_norm)
    return pl.pallas_call(
        kern,
        grid=(b, nt, nc),
        in_specs=in_specs,
        out_specs=tok,
        out_shape=jax.ShapeDtypeStruct((b, t, d), F32),
        scratch_shapes=[pltpu.VMEM((rows, d), BF16), pltpu.VMEM((tt, d), F32)],
        compiler_params=_params("arbitrary", "arbitrary", "arbitrary"),
        name="conv_ffn",
    )(*xs, mod, nw, w_up, w_up, wc, wc, bc, bc, w_down, nf)


def _sgu_kernel(x_ref, mod_ref, nw_ref, win_ref, lnw_ref, lnb_ref, ws_ref, bs_ref, wout_ref, o_ref):
    d = x_ref.shape[2]
    tt = x_ref.shape[1]
    e = wout_ref.shape[0]
    groups = ws_ref.shape[0]
    gw = e // groups
    shift = mod_ref[0, :, 0:d]
    scale = mod_ref[0, :, d:2 * d]
    gate = mod_ref[0, :, 2 * d:3 * d]
    x = x_ref[0]
    h = _norm_mod(x, nw_ref[...], shift, scale)
    z = _dot(h, win_ref[...])
    z = 0.5 * z * (1.0 + lax.erf(z * (1.0 / math.sqrt(2.0))))
    u = z[:, :e]
    v = z[:, e:]
    mu = jnp.mean(v, axis=-1, keepdims=True)
    vc = v - mu
    var = jnp.mean(vc * vc, axis=-1, keepdims=True)
    vn = (vc * lax.rsqrt(var + LN_EPS) * lnw_ref[...] + lnb_ref[...]).astype(BF16)
    chunks = []
    for p in range(tt // SGU_CHUNK):
        rows = slice(p * SGU_CHUNK, (p + 1) * SGU_CHUNK)
        parts = [jnp.dot(ws_ref[gi], vn[rows, gi * gw:(gi + 1) * gw], preferred_element_type=F32)
                 for gi in range(groups)]
        vm = jnp.concatenate(parts, axis=1) + bs_ref[...]
        chunks.append(u[rows] * vm)
    gated = jnp.concatenate(chunks, axis=0)
    out = _dot(gated, wout_ref[...])
    o_ref[0] = x + gate * out


def _sgu(x, mod, nw, w_in, ln_w, ln_b, w_s, b_full, w_out):
    b, t, d = x.shape
    tt = TOKEN_TILE
    tok = pl.BlockSpec((1, tt, d), lambda bi, i: (bi, i, 0))
    consts = (nw, w_in, ln_w, ln_b, w_s, b_full, w_out)
    return pl.pallas_call(
        _sgu_kernel,
        grid=(b, t // tt),
        in_specs=[tok, _mod_spec(mod)] + [_const_spec(c) for c in consts],
        out_specs=tok,
        out_shape=jax.ShapeDtypeStruct((b, t, d), F32),
        compiler_params=_params("arbitrary", "arbitrary"),
        name="sgu",
    )(x, mod, *consts)


def _lane_param(p, heads):
    m = p.reshape(heads // 2, 2, HEAD_DIM).transpose(2, 1, 0)
    m = jnp.broadcast_to(m[:, :, None, None, :], (HEAD_DIM, 2, 2, SCAN_BATCH, heads // 2))
    return m.reshape(HEAD_DIM, LANES)


def _state_to_scan(sf, sb):
    b, h = sf.shape[:2]
    g = b // SCAN_BATCH
    s = jnp.stack([sf, sb], axis=0).reshape(2, g, SCAN_BATCH, h // 2, 2, HEAD_DIM, HEAD_DIM)
    return s.transpose(1, 6, 5, 4, 0, 2, 3).reshape(g, HEAD_DIM, HEAD_DIM, LANES)


def _state_from_scan(s, heads):
    g = s.shape[0]
    s = s.reshape(g, HEAD_DIM, HEAD_DIM, 2, 2, SCAN_BATCH, heads // 2)
    s = s.transpose(4, 0, 5, 6, 3, 2, 1).reshape(2, g * SCAN_BATCH, heads, HEAD_DIM, HEAD_DIM)
    return s[0], s[1]


def kernel(x_prompt, x_sample, state_ctx_fwd, state_ctx_bwd, c, c_ctx, ada_w, ada_b, norm_mix, norm_ffn, ffn_up, ffn_conv, ffn_conv_b, ffn_down, norm_final, rw_mu, rw_wr, rw_wk, rw_wv, rw_wo, rw_w0, rw_w1, rw_w2, rw_a0, rw_a1, rw_a2, rw_g1, rw_g2, rw_kk, rw_ka, rw_rk, rw_lnx_w, rw_lnx_b, sg_in, sg_ln_w, sg_ln_b, sg_ws, sg_bs, sg_out):
    depth, d, _ = ada_w.shape
    heads = d // HEAD_DIM
    n_dec = c.shape[0]
    assert SCAN_BATCH * heads * 2 == LANES
    assert n_dec + 1 <= SUBLANES

    cond8 = jnp.concatenate(
        [c_ctx[None, :], c, jnp.zeros((SUBLANES - 1 - n_dec, d), F32)], axis=0)
    mod_all = _modulation(cond8, ada_w, ada_b)
    nf = norm_final.reshape(1, d)

    def run_stream(x, mod_rows, s0f, s0b, width, want_state):
        b, t, _ = x.shape
        new_f, new_b = [], []
        for i in range(depth):
            mod = mod_all[i, mod_rows[0]:mod_rows[1]][:, None, :]
            j = i // 2
            nw_mix = norm_mix[i].reshape(1, d)
            if i % 2 == 0:
                w1 = jnp.concatenate([rw_w1[j, 0], rw_w1[j, 1]], axis=1).astype(BF16)
                a1 = jnp.concatenate([rw_a1[j, 0], rw_a1[j, 1]], axis=1).astype(BF16)
                r, k, v, df, db, af, ab, g = _rw_proj(
                    x, mod, nw_mix, rw_mu[j],
                    rw_wr[j].astype(BF16), rw_wk[j].astype(BF16), rw_wv[j].astype(BF16),
                    w1, a1, rw_g1[j].astype(BF16),
                    rw_w2[j].astype(BF16), rw_a2[j].astype(BF16), rw_w0[j], rw_a0[j])
                s0 = _state_to_scan(s0f[:, j], s0b[:, j])
                ys, bvs, sfin = _scan(
                    r, k, v, df, db, af, ab, s0,
                    _lane_param(rw_kk[j], heads), _lane_param(rw_ka[j], heads),
                    _lane_param(rw_rk[j].reshape(-1), heads))
                lo, hi = _rw_post(ys, bvs, _lane_param(rw_lnx_w[j], heads),
                                  _lane_param(rw_lnx_b[j], heads), d)
                yn = jnp.concatenate([lo, hi], axis=1)
                x = _rw_out(x, yn, g, mod, rw_g2[j].astype(BF16), rw_wo[j].astype(BF16))
                if want_state:
                    sf, sb = _state_from_scan(sfin, heads)
                    new_f.append(sf)
                    new_b.append(sb)
            else:
                e = sg_out.shape[1]
                b_full = jnp.repeat(sg_bs[j].T, e // sg_bs.shape[1], axis=1)
                x = _sgu(x, mod, nw_mix, sg_in[j].astype(BF16),
                         sg_ln_w[j].reshape(1, e), sg_ln_b[j].reshape(1, e),
                         sg_ws[j].astype(BF16), b_full, sg_out[j].astype(BF16))
            x = _ffn(x, mod, norm_ffn[i].reshape(1, d), ffn_up[i].astype(BF16), ffn_conv[i],
                     ffn_conv_b[i], ffn_down[i].astype(BF16), nf,
                     width=width, final_norm=(i == depth - 1))
        if want_state:
            return x, jnp.stack(new_f, axis=1), jnp.stack(new_b, axis=1)
        return x, None, None

    bp = x_prompt.shape[0]
    n_rwkv = state_ctx_fwd.shape[1]
    zero_state = jnp.zeros((bp, n_rwkv, heads, HEAD_DIM, HEAD_DIM), F32)
    y_prompt, new_f, new_b = run_stream(x_prompt, (0, 1), zero_state, zero_state,
                                        x_prompt.shape[1], True)
    y_sample, _, _ = run_stream(x_sample, (1, 1 + n_dec), state_ctx_fwd, state_ctx_bwd,
                                GRID_W, False)
    return (y_prompt, y_sample, new_f, new_b)
```

```python
import functools
import math

import jax
import jax.numpy as jnp
from jax import lax
from jax.experimental import pallas as pl
from jax.experimental.pallas import tpu as pltpu

RMS_EPS = 1e-6
LN_EPS = 1e-5
GN_EPS = 64e-5
HEAD_DIM = 64
SGU_CHUNK = 128
GRID_W = 64

LANES = 128
SUBLANES = 8
VMEM_LIMIT = 56 * 1024 * 1024

TOKEN_TILE = 256
FFN_TILE = 512
FFN_CHANNEL_BLOCKS = 2
SCAN_BATCH = 4
SCAN_TCHUNK = 32
SCAN_ROWS = 64
SCAN_ACCS = 1
SCAN_SLOTS = 3
POST_TCHUNK = 16

BF16 = jnp.bfloat16
F32 = jnp.float32


def _params(*sem):
    return pltpu.CompilerParams(dimension_semantics=sem, vmem_limit_bytes=VMEM_LIMIT)


def _dot(a, b):
    return jnp.dot(a.astype(BF16), b, preferred_element_type=F32)


def _norm_mod(x, g, shift, scale):
    ms = jnp.mean(x * x, axis=-1, keepdims=True)
    return x * lax.rsqrt(ms + RMS_EPS) * g * (1.0 + scale) + shift


def _sigmoid(x):
    return 1.0 / (1.0 + jnp.exp(-x))


def _tree_sum(xs):
    while len(xs) > 1:
        xs = [xs[i] + xs[i + 1] for i in range(0, len(xs), 2)]
    return xs[0]


def _tile_major_shape(b, t, d):
    return (b, t // SUBLANES, d // LANES * SUBLANES, LANES)


def _put_tiles(ref, val):
    n = val.shape[0] // SUBLANES
    for p in range(val.shape[1] // LANES):
        ref[0, :, p * SUBLANES:(p + 1) * SUBLANES, :] = (
            val[:, p * LANES:(p + 1) * LANES].reshape(n, SUBLANES, LANES))


def _get_tiles(ref):
    _, n, rows, _ = ref.shape
    return jnp.concatenate(
        [ref[0, :, p * SUBLANES:(p + 1) * SUBLANES, :].reshape(n * SUBLANES, LANES)
         for p in range(rows // SUBLANES)], axis=1)


def _token_rows(u, pairs):
    return pl.ds(u, pairs, stride=SUBLANES)


def _mod_kernel(c_ref, w_ref, b_ref, o_ref):
    c = c_ref[...]
    sc = c * _sigmoid(c)
    o_ref[0] = jnp.dot(sc, w_ref[0], preferred_element_type=F32,
                       precision=lax.Precision.HIGHEST) + b_ref[0]


def _modulation(cond8, ada_w, ada_b):
    depth, d, n = ada_w.shape
    tn = n // 4
    return pl.pallas_call(
        _mod_kernel,
        grid=(depth, n // tn),
        in_specs=[
            pl.BlockSpec((SUBLANES, d), lambda l, j: (0, 0)),
            pl.BlockSpec((1, d, tn), lambda l, j: (l, 0, j)),
            pl.BlockSpec((1, 1, tn), lambda l, j: (l, 0, j)),
        ],
        out_specs=pl.BlockSpec((1, SUBLANES, tn), lambda l, j: (l, 0, j)),
        out_shape=jax.ShapeDtypeStruct((depth, SUBLANES, n), F32),
        compiler_params=_params("arbitrary", "arbitrary"),
        name="modulation",
    )(cond8, ada_w, ada_b.reshape(depth, 1, n))


def _mod_spec(mod):
    if mod.shape[0] == 1:
        return pl.BlockSpec((1, 1, mod.shape[2]), lambda b, i: (0, 0, 0))
    return pl.BlockSpec((1, 1, mod.shape[2]), lambda b, i: (b, 0, 0))


def _const_spec(a):
    nd = a.ndim
    return pl.BlockSpec(a.shape, lambda *_: (0,) * nd)


def _rw_proj_kernel(x_ref, xp_ref, xn_ref, mod_ref, nw_ref, mu_ref,
                    wr_ref, wk_ref, wv_ref, w1_ref, a1_ref, g1_ref,
                    w2_ref, a2_ref, w0_ref, a0_ref,
                    r_ref, k_ref, v_ref, df_ref, db_ref, af_ref, ab_ref, g_ref):
    i = pl.program_id(1)
    last = pl.num_programs(1) - 1
    d = x_ref.shape[2]
    tt = x_ref.shape[1]
    nw = nw_ref[...]
    shift = mod_ref[0, :, 0:d]
    scale = mod_ref[0, :, d:2 * d]
    h = _norm_mod(x_ref[0], nw, shift, scale)
    hp = _norm_mod(xp_ref[0], nw, shift, scale)[SUBLANES - 1:SUBLANES, :]
    hn = _norm_mod(xn_ref[0], nw, shift, scale)[0:1, :]
    hp = jnp.where(i == 0, 0.0, hp)
    hn = jnp.where(i == last, 0.0, hn)
    rows = lax.broadcasted_iota(jnp.int32, (tt, 1), 0)
    prev = jnp.where(rows == 0, hp, pltpu.roll(h, 1, axis=0))
    nxt = jnp.where(rows == tt - 1, hn, pltpu.roll(h, tt - 1, axis=0))
    dp = prev - h
    dn = nxt - h

    def shifted(idx):
        return h + dp * mu_ref[0, idx:idx + 1, :] + dn * mu_ref[1, idx:idx + 1, :]

    _put_tiles(r_ref, _dot(shifted(0), wr_ref[...]))
    _put_tiles(k_ref, _dot(shifted(2), wk_ref[...]))
    _put_tiles(v_ref, _dot(shifted(3), wv_ref[...]))
    g_ref[0] = _sigmoid(_dot(shifted(5), g1_ref[...]))
    lw = jnp.tanh(_dot(shifted(1), w1_ref[...]))
    la = _dot(shifted(4), a1_ref[...])
    lora = w2_ref.shape[1]
    decay_scale = math.exp(-0.5)
    for dirn, (d_ref, a_ref) in enumerate(((df_ref, af_ref), (db_ref, ab_ref))):
        w_raw = w0_ref[dirn:dirn + 1, :] + _dot(lw[:, dirn * lora:(dirn + 1) * lora], w2_ref[dirn])
        _put_tiles(d_ref, jnp.exp(-decay_scale * _sigmoid(w_raw)))
        a_raw = a0_ref[dirn:dirn + 1, :] + _dot(la[:, dirn * lora:(dirn + 1) * lora], a2_ref[dirn])
        _put_tiles(a_ref, _sigmoid(a_raw))


def _rw_proj(x, mod, nw, mu, wr, wk, wv, w1, a1, g1, w2, a2, w0, a0):
    b, t, d = x.shape
    tt = TOKEN_TILE
    nt = t // tt
    hb = tt // SUBLANES
    nhb = t // SUBLANES
    tok = pl.BlockSpec((1, tt, d), lambda bi, i: (bi, i, 0))
    prev = pl.BlockSpec((1, SUBLANES, d), lambda bi, i: (bi, jnp.maximum(i * hb - 1, 0), 0))
    nxt = pl.BlockSpec((1, SUBLANES, d), lambda bi, i: (bi, jnp.minimum((i + 1) * hb, nhb - 1), 0))
    consts = (nw, mu, wr, wk, wv, w1, a1, g1, w2, a2, w0, a0)
    gl = g1.shape[1]
    tiles = _tile_major_shape(b, t, d)
    tile_spec = pl.BlockSpec((1, tt // SUBLANES) + tiles[2:], lambda bi, i: (bi, i, 0, 0))
    out_shape = [jax.ShapeDtypeStruct(tiles, F32)] * 7 + [jax.ShapeDtypeStruct((b, t, gl), F32)]
    out_specs = [tile_spec] * 7 + [pl.BlockSpec((1, tt, gl), lambda bi, i: (bi, i, 0))]
    return pl.pallas_call(
        _rw_proj_kernel,
        grid=(b, nt),
        in_specs=[tok, prev, nxt, _mod_spec(mod)] + [_const_spec(c) for c in consts],
        out_specs=out_specs,
        out_shape=out_shape,
        compiler_params=_params("arbitrary", "arbitrary"),
        name="rw_proj",
    )(x, x, x, mod, *consts)


def _scan_kernel(rf_ref, rm_ref, kf_ref, km_ref, vf_ref, vm_ref, wf_ref, wm_ref, af_ref, am_ref,
                 s0_ref, kk_ref, ka_ref, rk_ref, y_ref, bv_ref, sfin_ref,
                 s_ref, ops_ref, sa_ref, stage_ref, cols_ref):
    c = pl.program_id(1)
    nb, ntb, pairs = rf_ref.shape[0], rf_ref.shape[1], rf_ref.shape[2] // SUBLANES
    tc = ntb * SUBLANES
    hd = s_ref.shape[0]

    @pl.when(c == 0)
    def _():
        s_ref[:, 0:hd, :] = s0_ref[0]

    operands = ((rf_ref, rm_ref), (kf_ref, km_ref), (vf_ref, vm_ref), (af_ref, am_ref),
                (wf_ref, wm_ref))

    def gather_rows(tb):
        half = tb % 2
        for op, (fwd_ref, bwd_ref) in enumerate(operands):
            for u in range(SUBLANES):
                for b in range(nb):
                    stage_ref[half, op, u, b * pairs:(b + 1) * pairs, :] = (
                        fwd_ref[b, tb, _token_rows(u, pairs), :])
                    stage_ref[half, op, u, (nb + b) * pairs:(nb + b + 1) * pairs, :] = (
                        bwd_ref[b, ntb - 1 - tb, _token_rows(SUBLANES - 1 - u, pairs), :])

    def transpose_rows(s):
        half = (s // SUBLANES) % 2
        u = s % SUBLANES
        for op in range(len(operands)):
            cols = stage_ref[half, op, u].T
            cols_ref[op] = jnp.concatenate([cols[0:hd], cols[hd:2 * hd]], axis=1)

    def prepare(s, slot):
        r = cols_ref[0]
        kraw = cols_ref[1]
        v = cols_ref[2]
        a = cols_ref[3]
        kkv = kraw * kk_ref[...]
        nrm = jnp.sqrt(jnp.sum(kkv * kkv, axis=0, keepdims=True))
        kk = kkv / jnp.maximum(nrm, 1e-12)
        kd = kraw * (1.0 + (a - 1.0) * ka_ref[...])
        ops_ref[slot, 0] = -kk
        ops_ref[slot, 1] = kk * a
        ops_ref[slot, 2] = kd
        ops_ref[slot, 3] = cols_ref[4]
        ops_ref[slot, 4] = r
        ops_ref[slot, 5] = v
        bv_ref[0, s] = v * jnp.sum(r * kd * rk_ref[...], axis=0, keepdims=True)

    def accumulate(acc, j, term):
        n = j % SCAN_ACCS
        acc[n] = term if acc[n] is None else acc[n] + term

    def first_sa(slot):
        for i0 in range(0, hd, SCAN_ROWS):
            rows = slice(i0, i0 + SCAN_ROWS)
            acc = [None] * SCAN_ACCS
            for j in range(hd):
                accumulate(acc, j, s_ref[j, rows, :] * ops_ref[slot, 0, j:j + 1, :])
            sa_ref[rows, :] = _tree_sum(acc)

    def advance(s, slot, nxt):
        for i0 in range(0, hd, SCAN_ROWS):
            rows = slice(i0, i0 + SCAN_ROWS)
            sa = sa_ref[rows, :]
            vr = ops_ref[slot, 5, rows, :]
            acc_y = [None] * SCAN_ACCS
            acc_a = [None] * SCAN_ACCS
            for j in range(hd):
                sn = (s_ref[j, rows, :] * ops_ref[slot, 3, j:j + 1, :]
                      + sa * ops_ref[slot, 1, j:j + 1, :] + vr * ops_ref[slot, 2, j:j + 1, :])
                s_ref[j, rows, :] = sn
                accumulate(acc_y, j, sn * ops_ref[slot, 4, j:j + 1, :])
                accumulate(acc_a, j, sn * ops_ref[nxt, 0, j:j + 1, :])
            y_ref[0, s, rows, :] = _tree_sum(acc_y)
            sa_ref[rows, :] = _tree_sum(acc_a)

    gather_rows(0)
    for s in range(2):
        transpose_rows(s)
        prepare(s, s)
    first_sa(0)

    def step(s, carry):
        ahead = jnp.minimum(s + 2, tc - 1)
        transpose_rows(ahead)
        advance(s, s % SCAN_SLOTS, (s + 1) % SCAN_SLOTS)
        prepare(ahead, (s + 2) % SCAN_SLOTS)
        return carry

    def token_block(tb, carry):
        gather_rows(jnp.minimum(tb + 1, ntb - 1))
        return lax.fori_loop(tb * SUBLANES, (tb + 1) * SUBLANES, step, carry)

    lax.fori_loop(0, ntb, token_block, 0)

    @pl.when(c == pl.num_programs(1) - 1)
    def _():
        sfin_ref[0] = s_ref[:, 0:hd, :]


def _scan(r, k, v, wf, wb, af, ab, s0, kk, ka, rk):
    b, tblocks, _, _ = r.shape
    t = tblocks * SUBLANES
    hd = HEAD_DIM
    g = b // SCAN_BATCH
    tc = SCAN_TCHUNK
    nt = t // tc
    blk = (SCAN_BATCH, tc // SUBLANES) + r.shape[2:]
    fwd = pl.BlockSpec(blk, lambda gi, c: (gi, c, 0, 0))
    bwd = pl.BlockSpec(blk, lambda gi, c: (gi, nt - 1 - c, 0, 0))
    step = pl.BlockSpec((1, tc, hd, LANES), lambda gi, c: (gi, c, 0, 0))
    state = pl.BlockSpec((1, hd, hd, LANES), lambda gi, c: (gi, 0, 0, 0))
    steps = jax.ShapeDtypeStruct((g, t, hd, LANES), F32)
    return pl.pallas_call(
        _scan_kernel,
        grid=(g, nt),
        in_specs=[fwd, bwd] * 5 + [state, _const_spec(kk), _const_spec(ka), _const_spec(rk)],
        out_specs=[step, step, state],
        out_shape=[steps, steps, jax.ShapeDtypeStruct((g, hd, hd, LANES), F32)],
        scratch_shapes=[pltpu.VMEM((hd, hd + SUBLANES, LANES), F32),
                        pltpu.VMEM((SCAN_SLOTS, 6, hd, LANES), F32),
                        pltpu.VMEM((hd, LANES), F32),
                        pltpu.VMEM((2, 5, SUBLANES, 2 * SCAN_BATCH * r.shape[2] // SUBLANES, LANES), F32),
                        pltpu.VMEM((5, hd, LANES), F32)],
        compiler_params=_params("arbitrary", "arbitrary"),
        name="wkv_scan",
    )(r, r, k, k, v, v, wf, wb, af, ab, s0, kk, ka, rk)


def _rw_post_kernel(y_ref, ym_ref, bv_ref, bvm_ref, lw_ref, lb_ref, lo_ref, hi_ref):
    tp = y_ref.shape[1]
    hd = y_ref.shape[2]
    nb, ntb, pairs = lo_ref.shape[0], lo_ref.shape[1], lo_ref.shape[2] // SUBLANES
    quarter = LANES // 4
    lane = lax.broadcasted_iota(jnp.int32, (hd, LANES), 1)
    fwd_lane = (lane % (2 * quarter)) < quarter

    def swap_dirs(x):
        return jnp.where(fwd_lane, pltpu.roll(x, LANES - quarter, axis=1), pltpu.roll(x, quarter, axis=1))

    def token_block(tb, carry):
        for u in range(SUBLANES):
            t = tb * SUBLANES + u
            m = tp - 1 - t
            y = y_ref[0, t] + swap_dirs(ym_ref[0, m])
            bonus = bv_ref[0, t] + swap_dirs(bvm_ref[0, m])
            mean = jnp.mean(y, axis=0, keepdims=True)
            yc = y - mean
            var = jnp.mean(yc * yc, axis=0, keepdims=True)
            yn = yc * lax.rsqrt(var + GN_EPS) * lw_ref[...] + lb_ref[...]
            cols = (yn + bonus).T
            rows = jnp.concatenate([cols[0:hd], cols[hd:2 * hd]], axis=1)
            for b in range(nb):
                lo_ref[b, tb, _token_rows(u, pairs), :] = rows[b * pairs:(b + 1) * pairs]
                hi_ref[b, ntb - 1 - tb, _token_rows(SUBLANES - 1 - u, pairs), :] = (
                    rows[(nb + b) * pairs:(nb + b + 1) * pairs])
        return carry

    lax.fori_loop(0, ntb, token_block, 0)


def _rw_post(y, bv, lw, lb, d):
    g, t, hd, _ = y.shape
    tp = POST_TCHUNK
    nb = t // tp
    cur = pl.BlockSpec((1, tp, hd, LANES), lambda gi, c: (gi, c, 0, 0))
    mir = pl.BlockSpec((1, tp, hd, LANES), lambda gi, c: (gi, nb - 1 - c, 0, 0))
    tiles = _tile_major_shape(g * SCAN_BATCH, t // 2, d)
    blk = (SCAN_BATCH, tp // SUBLANES) + tiles[2:]
    lo = pl.BlockSpec(blk, lambda gi, c: (gi, c, 0, 0))
    hi = pl.BlockSpec(blk, lambda gi, c: (gi, nb // 2 - 1 - c, 0, 0))
    half_tokens = jax.ShapeDtypeStruct(tiles, F32)
    return pl.pallas_call(
        _rw_post_kernel,
        grid=(g, nb // 2),
        in_specs=[cur, mir, cur, mir, _const_spec(lw), _const_spec(lb)],
        out_specs=[lo, hi],
        out_shape=[half_tokens, half_tokens],
        compiler_params=_params("arbitrary", "arbitrary"),
        name="rw_post",
    )(y, y, bv, bv, lw, lb)


def _rw_out_kernel(x_ref, yn_ref, g_ref, mod_ref, g2_ref, wo_ref, o_ref):
    d = x_ref.shape[2]
    gate = mod_ref[0, :, 2 * d:3 * d]
    g = _dot(g_ref[0], g2_ref[...])
    out = _dot(_get_tiles(yn_ref) * g, wo_ref[...])
    o_ref[0] = x_ref[0] + gate * out


def _rw_out(x, yn, g, mod, g2, wo):
    b, t, d = x.shape
    tt = TOKEN_TILE
    tok = pl.BlockSpec((1, tt, d), lambda bi, i: (bi, i, 0))
    tiles = pl.BlockSpec((1, tt // SUBLANES) + yn.shape[2:], lambda bi, i: (bi, i, 0, 0))
    gspec = pl.BlockSpec((1, tt, g.shape[2]), lambda bi, i: (bi, i, 0))
    return pl.pallas_call(
        _rw_out_kernel,
        grid=(b, t // tt),
        in_specs=[tok, tiles, gspec, _mod_spec(mod), _const_spec(g2), _const_spec(wo)],
        out_specs=tok,
        out_shape=jax.ShapeDtypeStruct((b, t, d), F32),
        compiler_params=_params("arbitrary", "arbitrary"),
        name="rw_out",
    )(x, yn, g, mod, g2, wo)


def _conv_taps(cur, up, down, w_ref, width, vertical):
    n = cur.shape[0]
    col = lax.broadcasted_iota(jnp.int32, (n, 1), 0) % width
    if vertical:
        ext = jnp.concatenate([up, cur, down], axis=0)
        slabs = [(0, ext[0:n]), (1, cur), (2, ext[2 * width:2 * width + n])]
    else:
        slabs = [(1, cur)]
    z = []
    for dc in range(3):
        acc = None
        for dr, rows in slabs:
            term = rows * w_ref[3 * dr + dc:3 * dr + dc + 1, :]
            acc = term if acc is None else acc + term
        z.append(acc)
    left = jnp.where(col != 0, pltpu.roll(z[0], 1, axis=0), 0.0)
    right = jnp.where(col != width - 1, pltpu.roll(z[2], n - 1, axis=0), 0.0)
    return left + z[1] + right


def _ffn_kernel(*refs, width, vertical, final_norm):
    if vertical:
        (x_ref, xup_ref, xdn_ref, mod_ref, nw_ref, wuv_ref, wug_ref,
         wcv_ref, wcg_ref, bv_ref, bg_ref, wd_ref, nf_ref, o_ref, h_ref, acc_ref) = refs
    else:
        (x_ref, mod_ref, nw_ref, wuv_ref, wug_ref,
         wcv_ref, wcg_ref, bv_ref, bg_ref, wd_ref, nf_ref, o_ref, h_ref, acc_ref) = refs
    i = pl.program_id(1)
    c = pl.program_id(2)
    tt, d = x_ref.shape[1], x_ref.shape[2]
    pad = width if vertical else 0

    @pl.when(c == 0)
    def _():
        shift = mod_ref[0, :, 3 * d:4 * d]
        scale = mod_ref[0, :, 4 * d:5 * d]
        h_ref[pad:pad + tt, :] = _norm_mod(x_ref[0], nw_ref[...], shift, scale).astype(BF16)
        if vertical:
            h_ref[0:pad, :] = _norm_mod(xup_ref[0], nw_ref[...], shift, scale).astype(BF16)
            h_ref[pad + tt:, :] = _norm_mod(xdn_ref[0], nw_ref[...], shift, scale).astype(BF16)

    top = i == 0
    bottom = i == pl.num_programs(1) - 1

    def conv(w_up_ref, w_conv_ref):
        up = jnp.dot(h_ref[...], w_up_ref[...], preferred_element_type=F32)
        if not vertical:
            return _conv_taps(up, None, None, w_conv_ref, width, vertical)
        above = jnp.where(top, 0.0, up[0:pad])
        below = jnp.where(bottom, 0.0, up[pad + tt:])
        return _conv_taps(up[pad:pad + tt], above, below, w_conv_ref, width, vertical)

    val = conv(wuv_ref, wcv_ref) + bv_ref[...]
    gat = conv(wug_ref, wcg_ref) + bg_ref[...]
    act = gat * _sigmoid(gat) * val
    part = _dot(act, wd_ref[...])

    @pl.when(c == 0)
    def _():
        acc_ref[...] = part

    @pl.when(c != 0)
    def _():
        acc_ref[...] += part

    @pl.when(c == pl.num_programs(2) - 1)
    def _():
        gate = mod_ref[0, :, 5 * d:6 * d]
        y = x_ref[0] + gate * acc_ref[...]
        if final_norm:
            ms = jnp.mean(y * y, axis=-1, keepdims=True)
            y = y * lax.rsqrt(ms + RMS_EPS) * nf_ref[...]
        o_ref[0] = y


def _ffn(x, mod, nw, w_up, w_conv, b_conv, w_down, nf, *, width, final_norm):
    b, t, d = x.shape
    f = w_down.shape[0]
    tt = min(FFN_TILE, t)
    nt = t // tt
    vertical = width < t
    ck = f // FFN_CHANNEL_BLOCKS
    nc = FFN_CHANNEL_BLOCKS
    wc = w_conv.reshape(9, 2 * f)
    bc = b_conv.reshape(1, 2 * f)
    hb = tt // width if vertical else 1
    nhb = t // width if vertical else 1

    if mod.shape[0] == 1:
        mspec = pl.BlockSpec((1, 1, mod.shape[2]), lambda bi, i, c: (0, 0, 0))
    else:
        mspec = pl.BlockSpec((1, 1, mod.shape[2]), lambda bi, i, c: (bi, 0, 0))
    tok = pl.BlockSpec((1, tt, d), lambda bi, i, c: (bi, i, 0))
    xs = [x]
    x_specs = [tok]
    if vertical:
        xs += [x, x]
        x_specs += [
            pl.BlockSpec((1, width, d), lambda bi, i, c: (bi, jnp.maximum(i * hb - 1, 0), 0)),
            pl.BlockSpec((1, width, d), lambda bi, i, c: (bi, jnp.minimum((i + 1) * hb, nhb - 1), 0)),
        ]
    in_specs = x_specs + [
        mspec,
        pl.BlockSpec(nw.shape, lambda bi, i, c: (0, 0)),
        pl.BlockSpec((d, ck), lambda bi, i, c: (0, c)),
        pl.BlockSpec((d, ck), lambda bi, i, c: (0, c + nc)),
        pl.BlockSpec((9, ck), lambda bi, i, c: (0, c)),
        pl.BlockSpec((9, ck), lambda bi, i, c: (0, c + nc)),
        pl.BlockSpec((1, ck), lambda bi, i, c: (0, c)),
        pl.BlockSpec((1, ck), lambda bi, i, c: (0, c + nc)),
        pl.BlockSpec((ck, d), lambda bi, i, c: (c, 0)),
        pl.BlockSpec(nf.shape, lambda bi, i, c: (0, 0)),
    ]
    rows = tt + 2 * width if vertical else tt
    kern = functools.partial(_ffn_kernel, width=width, vertical=vertical, final_norm=final_norm)
    return pl.pallas_call(
        kern,
        grid=(b, nt, nc),
        in_specs=in_specs,
        out_specs=tok,
        out_shape=jax.ShapeDtypeStruct((b, t, d), F32),
        scratch_shapes=[pltpu.VMEM((rows, d), BF16), pltpu.VMEM((tt, d), F32)],
        compiler_params=_params("arbitrary", "arbitrary", "arbitrary"),
        name="conv_ffn",
    )(*xs, mod, nw, w_up, w_up, wc, wc, bc, bc, w_down, nf)


def _sgu_kernel(x_ref, mod_ref, nw_ref, win_ref, lnw_ref, lnb_ref, ws_ref, bs_ref, wout_ref, o_ref):
    d = x_ref.shape[2]
    tt = x_ref.shape[1]
    e = wout_ref.shape[0]
    groups = ws_ref.shape[0]
    gw = e // groups
    shift = mod_ref[0, :, 0:d]
    scale = mod_ref[0, :, d:2 * d]
    gate = mod_ref[0, :, 2 * d:3 * d]
    x = x_ref[0]
    h = _norm_mod(x, nw_ref[...], shift, scale)
    z = _dot(h, win_ref[...])
    z = 0.5 * z * (1.0 + lax.erf(z * (1.0 / math.sqrt(2.0))))
    u = z[:, :e]
    v = z[:, e:]
    mu = jnp.mean(v, axis=-1, keepdims=True)
    vc = v - mu
    var = jnp.mean(vc * vc, axis=-1, keepdims=True)
    vn = (vc * lax.rsqrt(var + LN_EPS) * lnw_ref[...] + lnb_ref[...]).astype(BF16)
    chunks = []
    for p in range(tt // SGU_CHUNK):
        rows = slice(p * SGU_CHUNK, (p + 1) * SGU_CHUNK)
        parts = [jnp.dot(ws_ref[gi], vn[rows, gi * gw:(gi + 1) * gw], preferred_element_type=F32)
                 for gi in range(groups)]
        vm = jnp.concatenate(parts, axis=1) + bs_ref[...]
        chunks.append(u[rows] * vm)
    gated = jnp.concatenate(chunks, axis=0)
    out = _dot(gated, wout_ref[...])
    o_ref[0] = x + gate * out


def _sgu(x, mod, nw, w_in, ln_w, ln_b, w_s, b_full, w_out):
    b, t, d = x.shape
    tt = TOKEN_TILE
    tok = pl.BlockSpec((1, tt, d), lambda bi, i: (bi, i, 0))
    consts = (nw, w_in, ln_w, ln_b, w_s, b_full, w_out)
    return pl.pallas_call(
        _sgu_kernel,
        grid=(b, t // tt),
        in_specs=[tok, _mod_spec(mod)] + [_const_spec(c) for c in consts],
        out_specs=tok,
        out_shape=jax.ShapeDtypeStruct((b, t, d), F32),
        compiler_params=_params("arbitrary", "arbitrary"),
        name="sgu",
    )(x, mod, *consts)


def _lane_param(p, heads):
    m = p.reshape(heads // 2, 2, HEAD_DIM).transpose(2, 1, 0)
    m = jnp.broadcast_to(m[:, :, None, None, :], (HEAD_DIM, 2, 2, SCAN_BATCH, heads // 2))
    return m.reshape(HEAD_DIM, LANES)


def _state_to_scan(sf, sb):
    b, h = sf.shape[:2]
    g = b // SCAN_BATCH
    s = jnp.stack([sf, sb], axis=0).reshape(2, g, SCAN_BATCH, h // 2, 2, HEAD_DIM, HEAD_DIM)
    return s.transpose(1, 6, 5, 4, 0, 2, 3).reshape(g, HEAD_DIM, HEAD_DIM, LANES)


def _state_from_scan(s, heads):
    g = s.shape[0]
    s = s.reshape(g, HEAD_DIM, HEAD_DIM, 2, 2, SCAN_BATCH, heads // 2)
    s = s.transpose(4, 0, 5, 6, 3, 2, 1).reshape(2, g * SCAN_BATCH, heads, HEAD_DIM, HEAD_DIM)
    return s[0], s[1]


def kernel(x_prompt, x_sample, state_ctx_fwd, state_ctx_bwd, c, c_ctx, ada_w, ada_b, norm_mix, norm_ffn, ffn_up, ffn_conv, ffn_conv_b, ffn_down, norm_final, rw_mu, rw_wr, rw_wk, rw_wv, rw_wo, rw_w0, rw_w1, rw_w2, rw_a0, rw_a1, rw_a2, rw_g1, rw_g2, rw_kk, rw_ka, rw_rk, rw_lnx_w, rw_lnx_b, sg_in, sg_ln_w, sg_ln_b, sg_ws, sg_bs, sg_out):
    depth, d, _ = ada_w.shape
    heads = d // HEAD_DIM
    n_dec = c.shape[0]
    assert SCAN_BATCH * heads * 2 == LANES
    assert n_dec + 1 <= SUBLANES

    cond8 = jnp.concatenate(
        [c_ctx[None, :], c, jnp.zeros((SUBLANES - 1 - n_dec, d), F32)], axis=0)
    mod_all = _modulation(cond8, ada_w, ada_b)
    nf = norm_final.reshape(1, d)

    def run_stream(x, mod_rows, s0f, s0b, width, want_state):
        b, t, _ = x.shape
        new_f, new_b = [], []
        for i in range(depth):
            mod = mod_all[i, mod_rows[0]:mod_rows[1]][:, None, :]
            j = i // 2
            nw_mix = norm_mix[i].reshape(1, d)
            if i % 2 == 0:
                w1 = jnp.concatenate([rw_w1[j, 0], rw_w1[j, 1]], axis=1).astype(BF16)
                a1 = jnp.concatenate([rw_a1[j, 0], rw_a1[j, 1]], axis=1).astype(BF16)
                r, k, v, df, db, af, ab, g = _rw_proj(
                    x, mod, nw_mix, rw_mu[j],
                    rw_wr[j].astype(BF16), rw_wk[j].astype(BF16), rw_wv[j].astype(BF16),
                    w1, a1, rw_g1[j].astype(BF16),
                    rw_w2[j].astype(BF16), rw_a2[j].astype(BF16), rw_w0[j], rw_a0[j])
                s0 = _state_to_scan(s0f[:, j], s0b[:, j])
                ys, bvs, sfin = _scan(
                    r, k, v, df, db, af, ab, s0,
                    _lane_param(rw_kk[j], heads), _lane_param(rw_ka[j], heads),
                    _lane_param(rw_rk[j].reshape(-1), heads))
                lo, hi = _rw_post(ys, bvs, _lane_param(rw_lnx_w[j], heads),
                                  _lane_param(rw_lnx_b[j], heads), d)
                yn = jnp.concatenate([lo, hi], axis=1)
                x = _rw_out(x, yn, g, mod, rw_g2[j].astype(BF16), rw_wo[j].astype(BF16))
                if want_state:
                    sf, sb = _state_from_scan(sfin, heads)
                    new_f.append(sf)
                    new_b.append(sb)
            else:
                e = sg_out.shape[1]
                b_full = jnp.repeat(sg_bs[j].T, e // sg_bs.shape[1], axis=1)
                x = _sgu(x, mod, nw_mix, sg_in[j].astype(BF16),
                         sg_ln_w[j].reshape(1, e), sg_ln_b[j].reshape(1, e),
                         sg_ws[j].astype(BF16), b_full, sg_out[j].astype(BF16))
            x = _ffn(x, mod, norm_ffn[i].reshape(1, d), ffn_up[i].astype(BF16), ffn_conv[i],
                     ffn_conv_b[i], ffn_down[i].astype(BF16), nf,
                     width=width, final_norm=(i == depth - 1))
        if want_state:
            return x, jnp.stack(new_f, axis=1), jnp.stack(new_b, axis=1)
        return x, None, None

    bp = x_prompt.shape[0]
    n_rwkv = state_ctx_fwd.shape[1]
    zero_state = jnp.zeros((bp, n_rwkv, heads, HEAD_DIM, HEAD_DIM), F32)
    y_prompt, new_f, new_b = run_stream(x_prompt, (0, 1), zero_state, zero_state,
                                        x_prompt.shape[1], True)
    y_sample, _, _ = run_stream(x_sample, (1, 1 + n_dec), state_ctx_fwd, state_ctx_bwd,
                                GRID_W, False)
    return (y_prompt, y_sample, new_f, new_b)
```

```python
import functools
import math

import jax
import jax.numpy as jnp
from jax import lax
from jax.experimental import pallas as pl
from jax.experimental.pallas import tpu as pltpu

RMS_EPS = 1e-6
LN_EPS = 1e-5
GN_EPS = 64e-5
HEAD_DIM = 64
SGU_CHUNK = 128
GRID_W = 64

LANES = 128
SUBLANES = 8
VMEM_LIMIT = 56 * 1024 * 1024

TOKEN_TILE = 256
FFN_TILE = 512
FFN_CHANNEL_BLOCKS = 2
SCAN_BATCH = 4
SCAN_TCHUNK = 64
SCAN_ROWS = 64
SCAN_ACCS = 1
SCAN_SLOTS = 3
POST_TCHUNK = 16

BF16 = jnp.bfloat16
F32 = jnp.float32


def _params(*sem):
    return pltpu.CompilerParams(dimension_semantics=sem, vmem_limit_bytes=VMEM_LIMIT)


def _dot(a, b):
    return jnp.dot(a.astype(BF16), b, preferred_element_type=F32)


def _norm_mod(x, g, shift, scale):
    ms = jnp.mean(x * x, axis=-1, keepdims=True)
    return x * lax.rsqrt(ms + RMS_EPS) * g * (1.0 + scale) + shift


def _sigmoid(x):
    return 1.0 / (1.0 + jnp.exp(-x))


def _tree_sum(xs):
    while len(xs) > 1:
        xs = [xs[i] + xs[i + 1] for i in range(0, len(xs), 2)]
    return xs[0]


def _tile_major_shape(b, t, d):
    return (b, t // SUBLANES, d // LANES * SUBLANES, LANES)


def _put_tiles(ref, val):
    n = val.shape[0] // SUBLANES
    for p in range(val.shape[1] // LANES):
        ref[0, :, p * SUBLANES:(p + 1) * SUBLANES, :] = (
            val[:, p * LANES:(p + 1) * LANES].reshape(n, SUBLANES, LANES))


def _get_tiles(ref):
    _, n, rows, _ = ref.shape
    return jnp.concatenate(
        [ref[0, :, p * SUBLANES:(p + 1) * SUBLANES, :].reshape(n * SUBLANES, LANES)
         for p in range(rows // SUBLANES)], axis=1)


def _token_rows(u, pairs):
    return pl.ds(u, pairs, stride=SUBLANES)


def _mod_kernel(c_ref, w_ref, b_ref, o_ref):
    c = c_ref[...]
    sc = c * _sigmoid(c)
    o_ref[0] = jnp.dot(sc, w_ref[0], preferred_element_type=F32,
                       precision=lax.Precision.HIGHEST) + b_ref[0]


def _modulation(cond8, ada_w, ada_b):
    depth, d, n = ada_w.shape
    tn = n // 4
    return pl.pallas_call(
        _mod_kernel,
        grid=(depth, n // tn),
        in_specs=[
            pl.BlockSpec((SUBLANES, d), lambda l, j: (0, 0)),
            pl.BlockSpec((1, d, tn), lambda l, j: (l, 0, j)),
            pl.BlockSpec((1, 1, tn), lambda l, j: (l, 0, j)),
        ],
        out_specs=pl.BlockSpec((1, SUBLANES, tn), lambda l, j: (l, 0, j)),
        out_shape=jax.ShapeDtypeStruct((depth, SUBLANES, n), F32),
        compiler_params=_params("arbitrary", "arbitrary"),
        name="modulation",
    )(cond8, ada_w, ada_b.reshape(depth, 1, n))


def _mod_spec(mod):
    if mod.shape[0] == 1:
        return pl.BlockSpec((1, 1, mod.shape[2]), lambda b, i: (0, 0, 0))
    return pl.BlockSpec((1, 1, mod.shape[2]), lambda b, i: (b, 0, 0))


def _const_spec(a):
    nd = a.ndim
    return pl.BlockSpec(a.shape, lambda *_: (0,) * nd)


def _rw_proj_kernel(x_ref, xp_ref, xn_ref, mod_ref, nw_ref, mu_ref,
                    wr_ref, wk_ref, wv_ref, w1_ref, a1_ref, g1_ref,
                    w2_ref, a2_ref, w0_ref, a0_ref,
                    r_ref, k_ref, v_ref, df_ref, db_ref, af_ref, ab_ref, g_ref):
    i = pl.program_id(1)
    last = pl.num_programs(1) - 1
    d = x_ref.shape[2]
    tt = x_ref.shape[1]
    nw = nw_ref[...]
    shift = mod_ref[0, :, 0:d]
    scale = mod_ref[0, :, d:2 * d]
    h = _norm_mod(x_ref[0], nw, shift, scale)
    hp = _norm_mod(xp_ref[0], nw, shift, scale)[SUBLANES - 1:SUBLANES, :]
    hn = _norm_mod(xn_ref[0], nw, shift, scale)[0:1, :]
    hp = jnp.where(i == 0, 0.0, hp)
    hn = jnp.where(i == last, 0.0, hn)
    rows = lax.broadcasted_iota(jnp.int32, (tt, 1), 0)
    prev = jnp.where(rows == 0, hp, pltpu.roll(h, 1, axis=0))
    nxt = jnp.where(rows == tt - 1, hn, pltpu.roll(h, tt - 1, axis=0))
    dp = prev - h
    dn = nxt - h

    def shifted(idx):
        return h + dp * mu_ref[0, idx:idx + 1, :] + dn * mu_ref[1, idx:idx + 1, :]

    _put_tiles(r_ref, _dot(shifted(0), wr_ref[...]))
    _put_tiles(k_ref, _dot(shifted(2), wk_ref[...]))
    _put_tiles(v_ref, _dot(shifted(3), wv_ref[...]))
    g_ref[0] = _sigmoid(_dot(shifted(5), g1_ref[...]))
    lw = jnp.tanh(_dot(shifted(1), w1_ref[...]))
    la = _dot(shifted(4), a1_ref[...])
    lora = w2_ref.shape[1]
    decay_scale = math.exp(-0.5)
    for dirn, (d_ref, a_ref) in enumerate(((df_ref, af_ref), (db_ref, ab_ref))):
        w_raw = w0_ref[dirn:dirn + 1, :] + _dot(lw[:, dirn * lora:(dirn + 1) * lora], w2_ref[dirn])
        _put_tiles(d_ref, jnp.exp(-decay_scale * _sigmoid(w_raw)))
        a_raw = a0_ref[dirn:dirn + 1, :] + _dot(la[:, dirn * lora:(dirn + 1) * lora], a2_ref[dirn])
        _put_tiles(a_ref, _sigmoid(a_raw))


def _rw_proj(x, mod, nw, mu, wr, wk, wv, w1, a1, g1, w2, a2, w0, a0):
    b, t, d = x.shape
    tt = TOKEN_TILE
    nt = t // tt
    hb = tt // SUBLANES
    nhb = t // SUBLANES
    tok = pl.BlockSpec((1, tt, d), lambda bi, i: (bi, i, 0))
    prev = pl.BlockSpec((1, SUBLANES, d), lambda bi, i: (bi, jnp.maximum(i * hb - 1, 0), 0))
    nxt = pl.BlockSpec((1, SUBLANES, d), lambda bi, i: (bi, jnp.minimum((i + 1) * hb, nhb - 1), 0))
    consts = (nw, mu, wr, wk, wv, w1, a1, g1, w2, a2, w0, a0)
    gl = g1.shape[1]
    tiles = _tile_major_shape(b, t, d)
    tile_spec = pl.BlockSpec((1, tt // SUBLANES) + tiles[2:], lambda bi, i: (bi, i, 0, 0))
    out_shape = [jax.ShapeDtypeStruct(tiles, F32)] * 7 + [jax.ShapeDtypeStruct((b, t, gl), F32)]
    out_specs = [tile_spec] * 7 + [pl.BlockSpec((1, tt, gl), lambda bi, i: (bi, i, 0))]
    return pl.pallas_call(
        _rw_proj_kernel,
        grid=(b, nt),
        in_specs=[tok, prev, nxt, _mod_spec(mod)] + [_const_spec(c) for c in consts],
        out_specs=out_specs,
        out_shape=out_shape,
        compiler_params=_params("arbitrary", "arbitrary"),
        name="rw_proj",
    )(x, x, x, mod, *consts)


def _scan_kernel(rf_ref, rm_ref, kf_ref, km_ref, vf_ref, vm_ref, wf_ref, wm_ref, af_ref, am_ref,
                 s0_ref, kk_ref, ka_ref, rk_ref, y_ref, bv_ref, sfin_ref,
                 s_ref, ops_ref, sa_ref, stage_ref, cols_ref):
    c = pl.program_id(1)
    nb, ntb, pairs = rf_ref.shape[0], rf_ref.shape[1], rf_ref.shape[2] // SUBLANES
    tc = ntb * SUBLANES
    hd = s_ref.shape[0]

    @pl.when(c == 0)
    def _():
        s_ref[...] = s0_ref[0]

    operands = ((rf_ref, rm_ref), (kf_ref, km_ref), (vf_ref, vm_ref), (af_ref, am_ref),
                (wf_ref, wm_ref))

    def gather_rows(tb):
        half = tb % 2
        for op, (fwd_ref, bwd_ref) in enumerate(operands):
            for u in range(SUBLANES):
                for b in range(nb):
                    stage_ref[half, op, u, b * pairs:(b + 1) * pairs, :] = (
                        fwd_ref[b, tb, _token_rows(u, pairs), :])
                    stage_ref[half, op, u, (nb + b) * pairs:(nb + b + 1) * pairs, :] = (
                        bwd_ref[b, ntb - 1 - tb, _token_rows(SUBLANES - 1 - u, pairs), :])

    def transpose_rows(s):
        half = (s // SUBLANES) % 2
        u = s % SUBLANES
        for op in range(len(operands)):
            cols = stage_ref[half, op, u].T
            cols_ref[op] = jnp.concatenate([cols[0:hd], cols[hd:2 * hd]], axis=1)

    def prepare(s, slot):
        r = cols_ref[0]
        kraw = cols_ref[1]
        v = cols_ref[2]
        a = cols_ref[3]
        kkv = kraw * kk_ref[...]
        nrm = jnp.sqrt(jnp.sum(kkv * kkv, axis=0, keepdims=True))
        kk = kkv / jnp.maximum(nrm, 1e-12)
        kd = kraw * (1.0 + (a - 1.0) * ka_ref[...])
        p_prev = ops_ref[(s + SCAN_SLOTS - 1) % SCAN_SLOTS, 5]
        p = p_prev * cols_ref[4]
        inv_p = 1.0 / p
        ops_ref[slot, 0] = -kk * p_prev
        ops_ref[slot, 1] = kk * a * inv_p
        ops_ref[slot, 2] = kd * inv_p
        ops_ref[slot, 3] = r * p
        ops_ref[slot, 4] = v
        ops_ref[slot, 5] = p
        bv_ref[0, s] = v * jnp.sum(r * kd * rk_ref[...], axis=0, keepdims=True)

    def accumulate(acc, j, term):
        n = j % SCAN_ACCS
        acc[n] = term if acc[n] is None else acc[n] + term

    def first_sa(slot):
        for i0 in range(0, hd, SCAN_ROWS):
            rows = slice(i0, i0 + SCAN_ROWS)
            acc = [None] * SCAN_ACCS
            for j in range(hd):
                accumulate(acc, j, s_ref[j, rows, :] * ops_ref[slot, 0, j:j + 1, :])
            sa_ref[rows, :] = _tree_sum(acc)

    def advance(s, slot, nxt):
        for i0 in range(0, hd, SCAN_ROWS):
            rows = slice(i0, i0 + SCAN_ROWS)
            sa = sa_ref[rows, :]
            vr = ops_ref[slot, 4, rows, :]
            acc_y = [None] * SCAN_ACCS
            acc_a = [None] * SCAN_ACCS
            for j in range(hd):
                sn = (s_ref[j, rows, :]
                      + sa * ops_ref[slot, 1, j:j + 1, :] + vr * ops_ref[slot, 2, j:j + 1, :])
                s_ref[j, rows, :] = sn
                accumulate(acc_y, j, sn * ops_ref[slot, 3, j:j + 1, :])
                accumulate(acc_a, j, sn * ops_ref[nxt, 0, j:j + 1, :])
            y_ref[0, s, rows, :] = _tree_sum(acc_y)
            sa_ref[rows, :] = _tree_sum(acc_a)

    ops_ref[SCAN_SLOTS - 1, 5] = jnp.ones((hd, LANES), F32)
    gather_rows(0)
    for s in range(2):
        transpose_rows(s)
        prepare(s, s)
    first_sa(0)

    def step(s, carry):
        ahead = jnp.minimum(s + 2, tc - 1)
        transpose_rows(ahead)
        advance(s, s % SCAN_SLOTS, (s + 1) % SCAN_SLOTS)
        prepare(ahead, (s + 2) % SCAN_SLOTS)
        return carry

    def token_block(tb, carry):
        gather_rows(jnp.minimum(tb + 1, ntb - 1))
        return lax.fori_loop(tb * SUBLANES, (tb + 1) * SUBLANES, step, carry)

    lax.fori_loop(0, ntb, token_block, 0)

    for j in range(hd):
        s_ref[j] = s_ref[j] * ops_ref[(tc - 1) % SCAN_SLOTS, 5, j:j + 1, :]

    @pl.when(c == pl.num_programs(1) - 1)
    def _():
        sfin_ref[0] = s_ref[...]


def _scan(r, k, v, wf, wb, af, ab, s0, kk, ka, rk):
    b, tblocks, _, _ = r.shape
    t = tblocks * SUBLANES
    hd = HEAD_DIM
    g = b // SCAN_BATCH
    tc = SCAN_TCHUNK
    nt = t // tc
    blk = (SCAN_BATCH, tc // SUBLANES) + r.shape[2:]
    fwd = pl.BlockSpec(blk, lambda gi, c: (gi, c, 0, 0))
    bwd = pl.BlockSpec(blk, lambda gi, c: (gi, nt - 1 - c, 0, 0))
    step = pl.BlockSpec((1, tc, hd, LANES), lambda gi, c: (gi, c, 0, 0))
    state = pl.BlockSpec((1, hd, hd, LANES), lambda gi, c: (gi, 0, 0, 0))
    steps = jax.ShapeDtypeStruct((g, t, hd, LANES), F32)
    return pl.pallas_call(
        _scan_kernel,
        grid=(g, nt),
        in_specs=[fwd, bwd] * 5 + [state, _const_spec(kk), _const_spec(ka), _const_spec(rk)],
        out_specs=[step, step, state],
        out_shape=[steps, steps, jax.ShapeDtypeStruct((g, hd, hd, LANES), F32)],
        scratch_shapes=[pltpu.VMEM((hd, hd, LANES), F32),
                        pltpu.VMEM((SCAN_SLOTS, 6, hd, LANES), F32),
                        pltpu.VMEM((hd, LANES), F32),
                        pltpu.VMEM((2, 5, SUBLANES, 2 * SCAN_BATCH * r.shape[2] // SUBLANES, LANES), F32),
                        pltpu.VMEM((5, hd, LANES), F32)],
        compiler_params=_params("arbitrary", "arbitrary"),
        name="wkv_scan",
    )(r, r, k, k, v, v, wf, wb, af, ab, s0, kk, ka, rk)


def _rw_post_kernel(y_ref, ym_ref, bv_ref, bvm_ref, lw_ref, lb_ref, lo_ref, hi_ref):
    tp = y_ref.shape[1]
    hd = y_ref.shape[2]
    nb, ntb, pairs = lo_ref.shape[0], lo_ref.shape[1], lo_ref.shape[2] // SUBLANES
    quarter = LANES // 4
    lane = lax.broadcasted_iota(jnp.int32, (hd, LANES), 1)
    fwd_lane = (lane % (2 * quarter)) < quarter

    def swap_dirs(x):
        return jnp.where(fwd_lane, pltpu.roll(x, LANES - quarter, axis=1), pltpu.roll(x, quarter, axis=1))

    def token_block(tb, carry):
        for u in range(SUBLANES):
            t = tb * SUBLANES + u
            m = tp - 1 - t
            y = y_ref[0, t] + swap_dirs(ym_ref[0, m])
            bonus = bv_ref[0, t] + swap_dirs(bvm_ref[0, m])
            mean = jnp.mean(y, axis=0, keepdims=True)
            yc = y - mean
            var = jnp.mean(yc * yc, axis=0, keepdims=True)
            yn = yc * lax.rsqrt(var + GN_EPS) * lw_ref[...] + lb_ref[...]
            cols = (yn + bonus).T
            rows = jnp.concatenate([cols[0:hd], cols[hd:2 * hd]], axis=1)
            for b in range(nb):
                lo_ref[b, tb, _token_rows(u, pairs), :] = rows[b * pairs:(b + 1) * pairs]
                hi_ref[b, ntb - 1 - tb, _token_rows(SUBLANES - 1 - u, pairs), :] = (
                    rows[(nb + b) * pairs:(nb + b + 1) * pairs])
        return carry

    lax.fori_loop(0, ntb, token_block, 0)


def _rw_post(y, bv, lw, lb, d):
    g, t, hd, _ = y.shape
    tp = POST_TCHUNK
    nb = t // tp
    cur = pl.BlockSpec((1, tp, hd, LANES), lambda gi, c: (gi, c, 0, 0))
    mir = pl.BlockSpec((1, tp, hd, LANES), lambda gi, c: (gi, nb - 1 - c, 0, 0))
    tiles = _tile_major_shape(g * SCAN_BATCH, t // 2, d)
    blk = (SCAN_BATCH, tp // SUBLANES) + tiles[2:]
    lo = pl.BlockSpec(blk, lambda gi, c: (gi, c, 0, 0))
    hi = pl.BlockSpec(blk, lambda gi, c: (gi, nb // 2 - 1 - c, 0, 0))
    half_tokens = jax.ShapeDtypeStruct(tiles, F32)
    return pl.pallas_call(
        _rw_post_kernel,
        grid=(g, nb // 2),
        in_specs=[cur, mir, cur, mir, _const_spec(lw), _const_spec(lb)],
        out_specs=[lo, hi],
        out_shape=[half_tokens, half_tokens],
        compiler_params=_params("arbitrary", "arbitrary"),
        name="rw_post",
    )(y, y, bv, bv, lw, lb)


def _rw_out_kernel(x_ref, yn_ref, g_ref, mod_ref, g2_ref, wo_ref, o_ref):
    d = x_ref.shape[2]
    gate = mod_ref[0, :, 2 * d:3 * d]
    g = _dot(g_ref[0], g2_ref[...])
    out = _dot(_get_tiles(yn_ref) * g, wo_ref[...])
    o_ref[0] = x_ref[0] + gate * out


def _rw_out(x, yn, g, mod, g2, wo):
    b, t, d = x.shape
    tt = TOKEN_TILE
    tok = pl.BlockSpec((1, tt, d), lambda bi, i: (bi, i, 0))
    tiles = pl.BlockSpec((1, tt // SUBLANES) + yn.shape[2:], lambda bi, i: (bi, i, 0, 0))
    gspec = pl.BlockSpec((1, tt, g.shape[2]), lambda bi, i: (bi, i, 0))
    return pl.pallas_call(
        _rw_out_kernel,
        grid=(b, t // tt),
        in_specs=[tok, tiles, gspec, _mod_spec(mod), _const_spec(g2), _const_spec(wo)],
        out_specs=tok,
        out_shape=jax.ShapeDtypeStruct((b, t, d), F32),
        compiler_params=_params("arbitrary", "arbitrary"),
        name="rw_out",
    )(x, yn, g, mod, g2, wo)


def _conv_taps(cur, up, down, w_ref, width, vertical):
    n = cur.shape[0]
    col = lax.broadcasted_iota(jnp.int32, (n, 1), 0) % width
    if vertical:
        ext = jnp.concatenate([up, cur, down], axis=0)
        slabs = [(0, ext[0:n]), (1, cur), (2, ext[2 * width:2 * width + n])]
    else:
        slabs = [(1, cur)]
    z = []
    for dc in range(3):
        acc = None
        for dr, rows in slabs:
            term = rows * w_ref[3 * dr + dc:3 * dr + dc + 1, :]
            acc = term if acc is None else acc + term
        z.append(acc)
    left = jnp.where(col != 0, pltpu.roll(z[0], 1, axis=0), 0.0)
    right = jnp.where(col != width - 1, pltpu.roll(z[2], n - 1, axis=0), 0.0)
    return left + z[1] + right


def _ffn_kernel(*refs, width, vertical, final_norm):
    if vertical:
        (x_ref, xup_ref, xdn_ref, mod_ref, nw_ref, wuv_ref, wug_ref,
         wcv_ref, wcg_ref, bv_ref, bg_ref, wd_ref, nf_ref, o_ref, h_ref, acc_ref) = refs
    else:
        (x_ref, mod_ref, nw_ref, wuv_ref, wug_ref,
         wcv_ref, wcg_ref, bv_ref, bg_ref, wd_ref, nf_ref, o_ref, h_ref, acc_ref) = refs
    i = pl.program_id(1)
    c = pl.program_id(2)
    tt, d = x_ref.shape[1], x_ref.shape[2]
    pad = width if vertical else 0

    @pl.when(c == 0)
    def _():
        shift = mod_ref[0, :, 3 * d:4 * d]
        scale = mod_ref[0, :, 4 * d:5 * d]
        h_ref[pad:pad + tt, :] = _norm_mod(x_ref[0], nw_ref[...], shift, scale).astype(BF16)
        if vertical:
            h_ref[0:pad, :] = _norm_mod(xup_ref[0], nw_ref[...], shift, scale).astype(BF16)
            h_ref[pad + tt:, :] = _norm_mod(xdn_ref[0], nw_ref[...], shift, scale).astype(BF16)

    top = i == 0
    bottom = i == pl.num_programs(1) - 1

    def conv(w_up_ref, w_conv_ref):
        up = jnp.dot(h_ref[...], w_up_ref[...], preferred_element_type=F32)
        if not vertical:
            return _conv_taps(up, None, None, w_conv_ref, width, vertical)
        above = jnp.where(top, 0.0, up[0:pad])
        below = jnp.where(bottom, 0.0, up[pad + tt:])
        return _conv_taps(up[pad:pad + tt], above, below, w_conv_ref, width, vertical)

    val = conv(wuv_ref, wcv_ref) + bv_ref[...]
    gat = conv(wug_ref, wcg_ref) + bg_ref[...]
    act = gat * _sigmoid(gat) * val
    part = _dot(act, wd_ref[...])

    @pl.when(c == 0)
    def _():
        acc_ref[...] = part

    @pl.when(c != 0)
    def _():
        acc_ref[...] += part

    @pl.when(c == pl.num_programs(2) - 1)
    def _():
        gate = mod_ref[0, :, 5 * d:6 * d]
        y = x_ref[0] + gate * acc_ref[...]
        if final_norm:
            ms = jnp.mean(y * y, axis=-1, keepdims=True)
            y = y * lax.rsqrt(ms + RMS_EPS) * nf_ref[...]
        o_ref[0] = y


def _ffn(x, mod, nw, w_up, w_conv, b_conv, w_down, nf, *, width, final_norm):
    b, t, d = x.shape
    f = w_down.shape[0]
    tt = min(FFN_TILE, t)
    nt = t // tt
    vertical = width < t
    ck = f // FFN_CHANNEL_BLOCKS
    nc = FFN_CHANNEL_BLOCKS
    wc = w_conv.reshape(9, 2 * f)
    bc = b_conv.reshape(1, 2 * f)
    hb = tt // width if vertical else 1
    nhb = t // width if vertical else 1

    if mod.shape[0] == 1:
        mspec = pl.BlockSpec((1, 1, mod.shape[2]), lambda bi, i, c: (0, 0, 0))
    else:
        mspec = pl.BlockSpec((1, 1, mod.shape[2]), lambda bi, i, c: (bi, 0, 0))
    tok = pl.BlockSpec((1, tt, d), lambda bi, i, c: (bi, i, 0))
    xs = [x]
    x_specs = [tok]
    if vertical:
        xs += [x, x]
        x_specs += [
            pl.BlockSpec((1, width, d), lambda bi, i, c: (bi, jnp.maximum(i * hb - 1, 0), 0)),
            pl.BlockSpec((1, width, d), lambda bi, i, c: (bi, jnp.minimum((i + 1) * hb, nhb - 1), 0)),
        ]
    in_specs = x_specs + [
        mspec,
        pl.BlockSpec(nw.shape, lambda bi, i, c: (0, 0)),
        pl.BlockSpec((d, ck), lambda bi, i, c: (0, c)),
        pl.BlockSpec((d, ck), lambda bi, i, c: (0, c + nc)),
        pl.BlockSpec((9, ck), lambda bi, i, c: (0, c)),
        pl.BlockSpec((9, ck), lambda bi, i, c: (0, c + nc)),
        pl.BlockSpec((1, ck), lambda bi, i, c: (0, c)),
        pl.BlockSpec((1, ck), lambda bi, i, c: (0, c + nc)),
        pl.BlockSpec((ck, d), lambda bi, i, c: (c, 0)),
        pl.BlockSpec(nf.shape, lambda bi, i, c: (0, 0)),
    ]
    rows = tt + 2 * width if vertical else tt
    kern = functools.partial(_ffn_kernel, width=width, vertical=vertical, final_norm=final_norm)
    return pl.pallas_call(
        kern,
        grid=(b, nt, nc),
        in_specs=in_specs,
        out_specs=tok,
        out_shape=jax.ShapeDtypeStruct((b, t, d), F32),
        scratch_shapes=[pltpu.VMEM((rows, d), BF16), pltpu.VMEM((tt, d), F32)],
        compiler_params=_params("arbitrary", "arbitrary", "arbitrary"),
        name="conv_ffn",
    )(*xs, mod, nw, w_up, w_up, wc, wc, bc, bc, w_down, nf)


def _sgu_kernel(x_ref, mod_ref, nw_ref, win_ref, lnw_ref, lnb_ref, ws_ref, bs_ref, wout_ref, o_ref):
    d = x_ref.shape[2]
    tt = x_ref.shape[1]
    e = wout_ref.shape[0]
    groups = ws_ref.shape[0]
    gw = e // groups
    shift = mod_ref[0, :, 0:d]
    scale = mod_ref[0, :, d:2 * d]
    gate = mod_ref[0, :, 2 * d:3 * d]
    x = x_ref[0]
    h = _norm_mod(x, nw_ref[...], shift, scale)
    z = _dot(h, win_ref[...])
    z = 0.5 * z * (1.0 + lax.erf(z * (1.0 / math.sqrt(2.0))))
    u = z[:, :e]
    v = z[:, e:]
    mu = jnp.mean(v, axis=-1, keepdims=True)
    vc = v - mu
    var = jnp.mean(vc * vc, axis=-1, keepdims=True)
    vn = (vc * lax.rsqrt(var + LN_EPS) * lnw_ref[...] + lnb_ref[...]).astype(BF16)
    chunks = []
    for p in range(tt // SGU_CHUNK):
        rows = slice(p * SGU_CHUNK, (p + 1) * SGU_CHUNK)
        parts = [jnp.dot(ws_ref[gi], vn[rows, gi * gw:(gi + 1) * gw], preferred_element_type=F32)
                 for gi in range(groups)]
        vm = jnp.concatenate(parts, axis=1) + bs_ref[...]
        chunks.append(u[rows] * vm)
    gated = jnp.concatenate(chunks, axis=0)
    out = _dot(gated, wout_ref[...])
    o_ref[0] = x + gate * out


def _sgu(x, mod, nw, w_in, ln_w, ln_b, w_s, b_full, w_out):
    b, t, d = x.shape
    tt = TOKEN_TILE
    tok = pl.BlockSpec((1, tt, d), lambda bi, i: (bi, i, 0))
    consts = (nw, w_in, ln_w, ln_b, w_s, b_full, w_out)
    return pl.pallas_call(
        _sgu_kernel,
        grid=(b, t // tt),
        in_specs=[tok, _mod_spec(mod)] + [_const_spec(c) for c in consts],
        out_specs=tok,
        out_shape=jax.ShapeDtypeStruct((b, t, d), F32),
        compiler_params=_params("arbitrary", "arbitrary"),
        name="sgu",
    )(x, mod, *consts)


def _lane_param(p, heads):
    m = p.reshape(heads // 2, 2, HEAD_DIM).transpose(2, 1, 0)
    m = jnp.broadcast_to(m[:, :, None, None, :], (HEAD_DIM, 2, 2, SCAN_BATCH, heads // 2))
    return m.reshape(HEAD_DIM, LANES)


def _state_to_scan(sf, sb):
    b, h = sf.shape[:2]
    g = b // SCAN_BATCH
    s = jnp.stack([sf, sb], axis=0).reshape(2, g, SCAN_BATCH, h // 2, 2, HEAD_DIM, HEAD_DIM)
    return s.transpose(1, 6, 5, 4, 0, 2, 3).reshape(g, HEAD_DIM, HEAD_DIM, LANES)


def _state_from_scan(s, heads):
    g = s.shape[0]
    s = s.reshape(g, HEAD_DIM, HEAD_DIM, 2, 2, SCAN_BATCH, heads // 2)
    s = s.transpose(4, 0, 5, 6, 3, 2, 1).reshape(2, g * SCAN_BATCH, heads, HEAD_DIM, HEAD_DIM)
    return s[0], s[1]


def kernel(x_prompt, x_sample, state_ctx_fwd, state_ctx_bwd, c, c_ctx, ada_w, ada_b, norm_mix, norm_ffn, ffn_up, ffn_conv, ffn_conv_b, ffn_down, norm_final, rw_mu, rw_wr, rw_wk, rw_wv, rw_wo, rw_w0, rw_w1, rw_w2, rw_a0, rw_a1, rw_a2, rw_g1, rw_g2, rw_kk, rw_ka, rw_rk, rw_lnx_w, rw_lnx_b, sg_in, sg_ln_w, sg_ln_b, sg_ws, sg_bs, sg_out):
    depth, d, _ = ada_w.shape
    heads = d // HEAD_DIM
    n_dec = c.shape[0]
    assert SCAN_BATCH * heads * 2 == LANES
    assert n_dec + 1 <= SUBLANES

    cond8 = jnp.concatenate(
        [c_ctx[None, :], c, jnp.zeros((SUBLANES - 1 - n_dec, d), F32)], axis=0)
    mod_all = _modulation(cond8, ada_w, ada_b)
    nf = norm_final.reshape(1, d)

    def run_stream(x, mod_rows, s0f, s0b, width, want_state):
        b, t, _ = x.shape
        new_f, new_b = [], []
        for i in range(depth):
            mod = mod_all[i, mod_rows[0]:mod_rows[1]][:, None, :]
            j = i // 2
            nw_mix = norm_mix[i].reshape(1, d)
            if i % 2 == 0:
                w1 = jnp.concatenate([rw_w1[j, 0], rw_w1[j, 1]], axis=1).astype(BF16)
                a1 = jnp.concatenate([rw_a1[j, 0], rw_a1[j, 1]], axis=1).astype(BF16)
                r, k, v, df, db, af, ab, g = _rw_proj(
                    x, mod, nw_mix, rw_mu[j],
                    rw_wr[j].astype(BF16), rw_wk[j].astype(BF16), rw_wv[j].astype(BF16),
                    w1, a1, rw_g1[j].astype(BF16),
                    rw_w2[j].astype(BF16), rw_a2[j].astype(BF16), rw_w0[j], rw_a0[j])
                s0 = _state_to_scan(s0f[:, j], s0b[:, j])
                ys, bvs, sfin = _scan(
                    r, k, v, df, db, af, ab, s0,
                    _lane_param(rw_kk[j], heads), _lane_param(rw_ka[j], heads),
                    _lane_param(rw_rk[j].reshape(-1), heads))
                lo, hi = _rw_post(ys, bvs, _lane_param(rw_lnx_w[j], heads),
                                  _lane_param(rw_lnx_b[j], heads), d)
                yn = jnp.concatenate([lo, hi], axis=1)
                x = _rw_out(x, yn, g, mod, rw_g2[j].astype(BF16), rw_wo[j].astype(BF16))
                if want_state:
                    sf, sb = _state_from_scan(sfin, heads)
                    new_f.append(sf)
                    new_b.append(sb)
            else:
                e = sg_out.shape[1]
                b_full = jnp.repeat(sg_bs[j].T, e // sg_bs.shape[1], axis=1)
                x = _sgu(x, mod, nw_mix, sg_in[j].astype(BF16),
                         sg_ln_w[j].reshape(1, e), sg_ln_b[j].reshape(1, e),
                         sg_ws[j].astype(BF16), b_full, sg_out[j].astype(BF16))
            x = _ffn(x, mod, norm_ffn[i].reshape(1, d), ffn_up[i].astype(BF16), ffn_conv[i],
                     ffn_conv_b[i], ffn_down[i].astype(BF16), nf,
                     width=width, final_norm=(i == depth - 1))
        if want_state:
            return x, jnp.stack(new_f, axis=1), jnp.stack(new_b, axis=1)
        return x, None, None

    bp = x_prompt.shape[0]
    n_rwkv = state_ctx_fwd.shape[1]
    zero_state = jnp.zeros((bp, n_rwkv, heads, HEAD_DIM, HEAD_DIM), F32)
    y_prompt, new_f, new_b = run_stream(x_prompt, (0, 1), zero_state, zero_state,
                                        x_prompt.shape[1], True)
    y_sample, _, _ = run_stream(x_sample, (1, 1 + n_dec), state_ctx_fwd, state_ctx_bwd,
                                GRID_W, False)
    return (y_prompt, y_sample, new_f, new_b)
```

```python
import functools
import math

import jax
import jax.numpy as jnp
from jax import lax
from jax.experimental import pallas as pl
from jax.experimental.pallas import tpu as pltpu

RMS_EPS = 1e-6
LN_EPS = 1e-5
GN_EPS = 64e-5
HEAD_DIM = 64
SGU_CHUNK = 128
GRID_W = 64

LANES = 128
SUBLANES = 8
VMEM_LIMIT = 56 * 1024 * 1024

TOKEN_TILE = 256
FFN_TILE = 512
FFN_CHANNEL_BLOCKS = 2
SCAN_BATCH = 4
SCAN_TCHUNK = 64
SCAN_ROWS = 64
SCAN_ACCS = 1
SCAN_SLOTS = 3
POST_TCHUNK = 64

BF16 = jnp.bfloat16
F32 = jnp.float32


def _params(*sem):
    return pltpu.CompilerParams(dimension_semantics=sem, vmem_limit_bytes=VMEM_LIMIT)


def _dot(a, b):
    return jnp.dot(a.astype(BF16), b, preferred_element_type=F32)


def _norm_mod(x, g, shift, scale):
    ms = jnp.mean(x * x, axis=-1, keepdims=True)
    return x * lax.rsqrt(ms + RMS_EPS) * g * (1.0 + scale) + shift


def _sigmoid(x):
    return 1.0 / (1.0 + jnp.exp(-x))


def _tree_sum(xs):
    while len(xs) > 1:
        xs = [xs[i] + xs[i + 1] for i in range(0, len(xs), 2)]
    return xs[0]


def _tile_major_shape(b, t, d):
    return (b, t // SUBLANES, d // LANES * SUBLANES, LANES)


def _put_tiles(ref, val):
    n = val.shape[0] // SUBLANES
    for p in range(val.shape[1] // LANES):
        ref[0, :, p * SUBLANES:(p + 1) * SUBLANES, :] = (
            val[:, p * LANES:(p + 1) * LANES].reshape(n, SUBLANES, LANES))


def _get_tiles(ref):
    _, n, rows, _ = ref.shape
    return jnp.concatenate(
        [ref[0, :, p * SUBLANES:(p + 1) * SUBLANES, :].reshape(n * SUBLANES, LANES)
         for p in range(rows // SUBLANES)], axis=1)


def _token_rows(u, pairs):
    return pl.ds(u, pairs, stride=SUBLANES)


def _mod_kernel(c_ref, w_ref, b_ref, o_ref):
    c = c_ref[...]
    sc = c * _sigmoid(c)
    o_ref[0] = jnp.dot(sc, w_ref[0], preferred_element_type=F32,
                       precision=lax.Precision.HIGHEST) + b_ref[0]


def _modulation(cond8, ada_w, ada_b):
    depth, d, n = ada_w.shape
    tn = n // 4
    return pl.pallas_call(
        _mod_kernel,
        grid=(depth, n // tn),
        in_specs=[
            pl.BlockSpec((SUBLANES, d), lambda l, j: (0, 0)),
            pl.BlockSpec((1, d, tn), lambda l, j: (l, 0, j)),
            pl.BlockSpec((1, 1, tn), lambda l, j: (l, 0, j)),
        ],
        out_specs=pl.BlockSpec((1, SUBLANES, tn), lambda l, j: (l, 0, j)),
        out_shape=jax.ShapeDtypeStruct((depth, SUBLANES, n), F32),
        compiler_params=_params("arbitrary", "arbitrary"),
        name="modulation",
    )(cond8, ada_w, ada_b.reshape(depth, 1, n))


def _mod_spec(mod):
    if mod.shape[0] == 1:
        return pl.BlockSpec((1, 1, mod.shape[2]), lambda b, i: (0, 0, 0))
    return pl.BlockSpec((1, 1, mod.shape[2]), lambda b, i: (b, 0, 0))


def _const_spec(a):
    nd = a.ndim
    return pl.BlockSpec(a.shape, lambda *_: (0,) * nd)


def _rw_proj_kernel(x_ref, xp_ref, xn_ref, mod_ref, nw_ref, mu_ref,
                    wr_ref, wk_ref, wv_ref, w1_ref, a1_ref, g1_ref,
                    w2_ref, a2_ref, w0_ref, a0_ref,
                    r_ref, k_ref, v_ref, df_ref, db_ref, af_ref, ab_ref, g_ref):
    i = pl.program_id(1)
    last = pl.num_programs(1) - 1
    d = x_ref.shape[2]
    tt = x_ref.shape[1]
    nw = nw_ref[...]
    shift = mod_ref[0, :, 0:d]
    scale = mod_ref[0, :, d:2 * d]
    h = _norm_mod(x_ref[0], nw, shift, scale)
    hp = _norm_mod(xp_ref[0], nw, shift, scale)[SUBLANES - 1:SUBLANES, :]
    hn = _norm_mod(xn_ref[0], nw, shift, scale)[0:1, :]
    hp = jnp.where(i == 0, 0.0, hp)
    hn = jnp.where(i == last, 0.0, hn)
    rows = lax.broadcasted_iota(jnp.int32, (tt, 1), 0)
    prev = jnp.where(rows == 0, hp, pltpu.roll(h, 1, axis=0))
    nxt = jnp.where(rows == tt - 1, hn, pltpu.roll(h, tt - 1, axis=0))
    dp = prev - h
    dn = nxt - h

    def shifted(idx):
        return h + dp * mu_ref[0, idx:idx + 1, :] + dn * mu_ref[1, idx:idx + 1, :]

    _put_tiles(r_ref, _dot(shifted(0), wr_ref[...]))
    _put_tiles(k_ref, _dot(shifted(2), wk_ref[...]))
    _put_tiles(v_ref, _dot(shifted(3), wv_ref[...]))
    g_ref[0] = _sigmoid(_dot(shifted(5), g1_ref[...]))
    lw = jnp.tanh(_dot(shifted(1), w1_ref[...]))
    la = _dot(shifted(4), a1_ref[...])
    lora = w2_ref.shape[1]
    decay_scale = math.exp(-0.5)
    for dirn, (d_ref, a_ref) in enumerate(((df_ref, af_ref), (db_ref, ab_ref))):
        w_raw = w0_ref[dirn:dirn + 1, :] + _dot(lw[:, dirn * lora:(dirn + 1) * lora], w2_ref[dirn])
        _put_tiles(d_ref, jnp.exp(-decay_scale * _sigmoid(w_raw)))
        a_raw = a0_ref[dirn:dirn + 1, :] + _dot(la[:, dirn * lora:(dirn + 1) * lora], a2_ref[dirn])
        _put_tiles(a_ref, _sigmoid(a_raw))


def _rw_proj(x, mod, nw, mu, wr, wk, wv, w1, a1, g1, w2, a2, w0, a0):
    b, t, d = x.shape
    tt = TOKEN_TILE
    nt = t // tt
    hb = tt // SUBLANES
    nhb = t // SUBLANES
    tok = pl.BlockSpec((1, tt, d), lambda bi, i: (bi, i, 0))
    prev = pl.BlockSpec((1, SUBLANES, d), lambda bi, i: (bi, jnp.maximum(i * hb - 1, 0), 0))
    nxt = pl.BlockSpec((1, SUBLANES, d), lambda bi, i: (bi, jnp.minimum((i + 1) * hb, nhb - 1), 0))
    consts = (nw, mu, wr, wk, wv, w1, a1, g1, w2, a2, w0, a0)
    gl = g1.shape[1]
    tiles = _tile_major_shape(b, t, d)
    tile_spec = pl.BlockSpec((1, tt // SUBLANES) + tiles[2:], lambda bi, i: (bi, i, 0, 0))
    out_shape = [jax.ShapeDtypeStruct(tiles, F32)] * 7 + [jax.ShapeDtypeStruct((b, t, gl), F32)]
    out_specs = [tile_spec] * 7 + [pl.BlockSpec((1, tt, gl), lambda bi, i: (bi, i, 0))]
    return pl.pallas_call(
        _rw_proj_kernel,
        grid=(b, nt),
        in_specs=[tok, prev, nxt, _mod_spec(mod)] + [_const_spec(c) for c in consts],
        out_specs=out_specs,
        out_shape=out_shape,
        compiler_params=_params("arbitrary", "arbitrary"),
        name="rw_proj",
    )(x, x, x, mod, *consts)


def _scan_kernel(rf_ref, rm_ref, kf_ref, km_ref, vf_ref, vm_ref, wf_ref, wm_ref, af_ref, am_ref,
                 s0_ref, kk_ref, ka_ref, rk_ref, y_ref, bv_ref, sfin_ref,
                 s_ref, ops_ref, sa_ref, stage_ref, cols_ref):
    c = pl.program_id(1)
    nb, ntb, pairs = rf_ref.shape[0], rf_ref.shape[1], rf_ref.shape[2] // SUBLANES
    tc = ntb * SUBLANES
    hd = s_ref.shape[0]

    @pl.when(c == 0)
    def _():
        s_ref[...] = s0_ref[0]

    operands = ((rf_ref, rm_ref), (kf_ref, km_ref), (vf_ref, vm_ref), (af_ref, am_ref),
                (wf_ref, wm_ref))

    def gather_rows(tb):
        half = tb % 2
        for op, (fwd_ref, bwd_ref) in enumerate(operands):
            for u in range(SUBLANES):
                for b in range(nb):
                    stage_ref[half, op, u, b * pairs:(b + 1) * pairs, :] = (
                        fwd_ref[b, tb, _token_rows(u, pairs), :])
                    stage_ref[half, op, u, (nb + b) * pairs:(nb + b + 1) * pairs, :] = (
                        bwd_ref[b, ntb - 1 - tb, _token_rows(SUBLANES - 1 - u, pairs), :])

    def transpose_rows(s):
        half = (s // SUBLANES) % 2
        u = s % SUBLANES
        for op in range(len(operands)):
            cols = stage_ref[half, op, u].T
            cols_ref[op] = jnp.concatenate([cols[0:hd], cols[hd:2 * hd]], axis=1)

    def prepare(s, slot):
        r = cols_ref[0]
        kraw = cols_ref[1]
        v = cols_ref[2]
        a = cols_ref[3]
        kkv = kraw * kk_ref[...]
        nrm = jnp.sqrt(jnp.sum(kkv * kkv, axis=0, keepdims=True))
        kk = kkv / jnp.maximum(nrm, 1e-12)
        kd = kraw * (1.0 + (a - 1.0) * ka_ref[...])
        p_prev = ops_ref[(s + SCAN_SLOTS - 1) % SCAN_SLOTS, 5]
        p = p_prev * cols_ref[4]
        inv_p = 1.0 / p
        ops_ref[slot, 0] = -kk * p_prev
        ops_ref[slot, 1] = kk * a * inv_p
        ops_ref[slot, 2] = kd * inv_p
        ops_ref[slot, 3] = r * p
        ops_ref[slot, 4] = v
        ops_ref[slot, 5] = p
        bv_ref[0, s] = v * jnp.sum(r * kd * rk_ref[...], axis=0, keepdims=True)

    def accumulate(acc, j, term):
        n = j % SCAN_ACCS
        acc[n] = term if acc[n] is None else acc[n] + term

    def first_sa(slot):
        for i0 in range(0, hd, SCAN_ROWS):
            rows = slice(i0, i0 + SCAN_ROWS)
            acc = [None] * SCAN_ACCS
            for j in range(hd):
                accumulate(acc, j, s_ref[j, rows, :] * ops_ref[slot, 0, j:j + 1, :])
            sa_ref[rows, :] = _tree_sum(acc)

    def advance(s, slot, nxt):
        for i0 in range(0, hd, SCAN_ROWS):
            rows = slice(i0, i0 + SCAN_ROWS)
            sa = sa_ref[rows, :]
            vr = ops_ref[slot, 4, rows, :]
            acc_y = [None] * SCAN_ACCS
            acc_a = [None] * SCAN_ACCS
            for j in range(hd):
                sn = (s_ref[j, rows, :]
                      + sa * ops_ref[slot, 1, j:j + 1, :] + vr * ops_ref[slot, 2, j:j + 1, :])
                s_ref[j, rows, :] = sn
                accumulate(acc_y, j, sn * ops_ref[slot, 3, j:j + 1, :])
                accumulate(acc_a, j, sn * ops_ref[nxt, 0, j:j + 1, :])
            y_ref[0, s, rows, :] = _tree_sum(acc_y)
            sa_ref[rows, :] = _tree_sum(acc_a)

    ops_ref[SCAN_SLOTS - 1, 5] = jnp.ones((hd, LANES), F32)
    gather_rows(0)
    for s in range(2):
        transpose_rows(s)
        prepare(s, s)
    first_sa(0)

    def step(s, carry):
        ahead = jnp.minimum(s + 2, tc - 1)
        transpose_rows(ahead)
        advance(s, s % SCAN_SLOTS, (s + 1) % SCAN_SLOTS)
        prepare(ahead, (s + 2) % SCAN_SLOTS)
        return carry

    def token_block(tb, carry):
        gather_rows(jnp.minimum(tb + 1, ntb - 1))
        return lax.fori_loop(tb * SUBLANES, (tb + 1) * SUBLANES, step, carry)

    lax.fori_loop(0, ntb, token_block, 0)

    for j in range(hd):
        s_ref[j] = s_ref[j] * ops_ref[(tc - 1) % SCAN_SLOTS, 5, j:j + 1, :]

    @pl.when(c == pl.num_programs(1) - 1)
    def _():
        sfin_ref[0] = s_ref[...]


def _scan(r, k, v, wf, wb, af, ab, s0, kk, ka, rk):
    b, tblocks, _, _ = r.shape
    t = tblocks * SUBLANES
    hd = HEAD_DIM
    g = b // SCAN_BATCH
    tc = SCAN_TCHUNK
    nt = t // tc
    blk = (SCAN_BATCH, tc // SUBLANES) + r.shape[2:]
    fwd = pl.BlockSpec(blk, lambda gi, c: (gi, c, 0, 0))
    bwd = pl.BlockSpec(blk, lambda gi, c: (gi, nt - 1 - c, 0, 0))
    step = pl.BlockSpec((1, tc, hd, LANES), lambda gi, c: (gi, c, 0, 0))
    state = pl.BlockSpec((1, hd, hd, LANES), lambda gi, c: (gi, 0, 0, 0))
    steps = jax.ShapeDtypeStruct((g, t, hd, LANES), F32)
    return pl.pallas_call(
        _scan_kernel,
        grid=(g, nt),
        in_specs=[fwd, bwd] * 5 + [state, _const_spec(kk), _const_spec(ka), _const_spec(rk)],
        out_specs=[step, step, state],
        out_shape=[steps, steps, jax.ShapeDtypeStruct((g, hd, hd, LANES), F32)],
        scratch_shapes=[pltpu.VMEM((hd, hd, LANES), F32),
                        pltpu.VMEM((SCAN_SLOTS, 6, hd, LANES), F32),
                        pltpu.VMEM((hd, LANES), F32),
                        pltpu.VMEM((2, 5, SUBLANES, 2 * SCAN_BATCH * r.shape[2] // SUBLANES, LANES), F32),
                        pltpu.VMEM((5, hd, LANES), F32)],
        compiler_params=_params("arbitrary", "arbitrary"),
        name="wkv_scan",
    )(r, r, k, k, v, v, wf, wb, af, ab, s0, kk, ka, rk)


def _rw_post_kernel(y_ref, ym_ref, bv_ref, bvm_ref, lw_ref, lb_ref, lo_ref, hi_ref):
    tp = y_ref.shape[1]
    hd = y_ref.shape[2]
    nb, ntb, pairs = lo_ref.shape[0], lo_ref.shape[1], lo_ref.shape[2] // SUBLANES
    quarter = LANES // 4
    lane = lax.broadcasted_iota(jnp.int32, (hd, LANES), 1)
    fwd_lane = (lane % (2 * quarter)) < quarter

    def swap_dirs(x):
        return jnp.where(fwd_lane, pltpu.roll(x, LANES - quarter, axis=1), pltpu.roll(x, quarter, axis=1))

    def token_block(tb, carry):
        ts = [tb * SUBLANES + u for u in range(SUBLANES)]
        ys = [y_ref[0, t] + swap_dirs(ym_ref[0, tp - 1 - t]) for t in ts]
        bonuses = [bv_ref[0, t] + swap_dirs(bvm_ref[0, tp - 1 - t]) for t in ts]
        outs = []
        for y, bonus in zip(ys, bonuses):
            mean = jnp.mean(y, axis=0, keepdims=True)
            yc = y - mean
            var = jnp.mean(yc * yc, axis=0, keepdims=True)
            yn = yc * lax.rsqrt(var + GN_EPS) * lw_ref[...] + lb_ref[...]
            outs.append(yn + bonus)
        cols = [o.T for o in outs]
        tiles = [jnp.concatenate([c[0:hd], c[hd:2 * hd]], axis=1) for c in cols]
        for u, rows in enumerate(tiles):
            for b in range(nb):
                lo_ref[b, tb, _token_rows(u, pairs), :] = rows[b * pairs:(b + 1) * pairs]
                hi_ref[b, ntb - 1 - tb, _token_rows(SUBLANES - 1 - u, pairs), :] = (
                    rows[(nb + b) * pairs:(nb + b + 1) * pairs])
        return carry

    lax.fori_loop(0, ntb, token_block, 0)


def _rw_post(y, bv, lw, lb, d):
    g, t, hd, _ = y.shape
    tp = POST_TCHUNK
    nb = t // tp
    cur = pl.BlockSpec((1, tp, hd, LANES), lambda gi, c: (gi, c, 0, 0))
    mir = pl.BlockSpec((1, tp, hd, LANES), lambda gi, c: (gi, nb - 1 - c, 0, 0))
    tiles = _tile_major_shape(g * SCAN_BATCH, t // 2, d)
    blk = (SCAN_BATCH, tp // SUBLANES) + tiles[2:]
    lo = pl.BlockSpec(blk, lambda gi, c: (gi, c, 0, 0))
    hi = pl.BlockSpec(blk, lambda gi, c: (gi, nb // 2 - 1 - c, 0, 0))
    half_tokens = jax.ShapeDtypeStruct(tiles, F32)
    return pl.pallas_call(
        _rw_post_kernel,
        grid=(g, nb // 2),
        in_specs=[cur, mir, cur, mir, _const_spec(lw), _const_spec(lb)],
        out_specs=[lo, hi],
        out_shape=[half_tokens, half_tokens],
        compiler_params=_params("arbitrary", "arbitrary"),
        name="rw_post",
    )(y, y, bv, bv, lw, lb)


def _rw_out_kernel(x_ref, yn_ref, g_ref, mod_ref, g2_ref, wo_ref, o_ref):
    d = x_ref.shape[2]
    gate = mod_ref[0, :, 2 * d:3 * d]
    g = _dot(g_ref[0], g2_ref[...])
    out = _dot(_get_tiles(yn_ref) * g, wo_ref[...])
    o_ref[0] = x_ref[0] + gate * out


def _rw_out(x, yn, g, mod, g2, wo):
    b, t, d = x.shape
    tt = TOKEN_TILE
    tok = pl.BlockSpec((1, tt, d), lambda bi, i: (bi, i, 0))
    tiles = pl.BlockSpec((1, tt // SUBLANES) + yn.shape[2:], lambda bi, i: (bi, i, 0, 0))
    gspec = pl.BlockSpec((1, tt, g.shape[2]), lambda bi, i: (bi, i, 0))
    return pl.pallas_call(
        _rw_out_kernel,
        grid=(b, t // tt),
        in_specs=[tok, tiles, gspec, _mod_spec(mod), _const_spec(g2), _const_spec(wo)],
        out_specs=tok,
        out_shape=jax.ShapeDtypeStruct((b, t, d), F32),
        compiler_params=_params("arbitrary", "arbitrary"),
        name="rw_out",
    )(x, yn, g, mod, g2, wo)


def _conv_taps(cur, up, down, w_ref, width, vertical):
    n = cur.shape[0]
    col = lax.broadcasted_iota(jnp.int32, (n, 1), 0) % width
    if vertical:
        ext = jnp.concatenate([up, cur, down], axis=0)
        slabs = [(0, ext[0:n]), (1, cur), (2, ext[2 * width:2 * width + n])]
    else:
        slabs = [(1, cur)]
    z = []
    for dc in range(3):
        acc = None
        for dr, rows in slabs:
            term = rows * w_ref[3 * dr + dc:3 * dr + dc + 1, :]
            acc = term if acc is None else acc + term
        z.append(acc)
    left = jnp.where(col != 0, pltpu.roll(z[0], 1, axis=0), 0.0)
    right = jnp.where(col != width - 1, pltpu.roll(z[2], n - 1, axis=0), 0.0)
    return left + z[1] + right


def _ffn_kernel(*refs, width, vertical, final_norm):
    if vertical:
        (x_ref, xup_ref, xdn_ref, mod_ref, nw_ref, wuv_ref, wug_ref,
         wcv_ref, wcg_ref, bv_ref, bg_ref, wd_ref, nf_ref, o_ref, h_ref, acc_ref) = refs
    else:
        (x_ref, mod_ref, nw_ref, wuv_ref, wug_ref,
         wcv_ref, wcg_ref, bv_ref, bg_ref, wd_ref, nf_ref, o_ref, h_ref, acc_ref) = refs
    i = pl.program_id(1)
    c = pl.program_id(2)
    tt, d = x_ref.shape[1], x_ref.shape[2]
    pad = width if vertical else 0

    @pl.when(c == 0)
    def _():
        shift = mod_ref[0, :, 3 * d:4 * d]
        scale = mod_ref[0, :, 4 * d:5 * d]
        h_ref[pad:pad + tt, :] = _norm_mod(x_ref[0], nw_ref[...], shift, scale).astype(BF16)
        if vertical:
            h_ref[0:pad, :] = _norm_mod(xup_ref[0], nw_ref[...], shift, scale).astype(BF16)
            h_ref[pad + tt:, :] = _norm_mod(xdn_ref[0], nw_ref[...], shift, scale).astype(BF16)

    top = i == 0
    bottom = i == pl.num_programs(1) - 1

    def conv(w_up_ref, w_conv_ref):
        up = jnp.dot(h_ref[...], w_up_ref[...], preferred_element_type=F32)
        if not vertical:
            return _conv_taps(up, None, None, w_conv_ref, width, vertical)
        above = jnp.where(top, 0.0, up[0:pad])
        below = jnp.where(bottom, 0.0, up[pad + tt:])
        return _conv_taps(up[pad:pad + tt], above, below, w_conv_ref, width, vertical)

    val = conv(wuv_ref, wcv_ref) + bv_ref[...]
    gat = conv(wug_ref, wcg_ref) + bg_ref[...]
    act = gat * _sigmoid(gat) * val
    part = _dot(act, wd_ref[...])

    @pl.when(c == 0)
    def _():
        acc_ref[...] = part

    @pl.when(c != 0)
    def _():
        acc_ref[...] += part

    @pl.when(c == pl.num_programs(2) - 1)
    def _():
        gate = mod_ref[0, :, 5 * d:6 * d]
        y = x_ref[0] + gate * acc_ref[...]
        if final_norm:
            ms = jnp.mean(y * y, axis=-1, keepdims=True)
            y = y * lax.rsqrt(ms + RMS_EPS) * nf_ref[...]
        o_ref[0] = y


def _ffn(x, mod, nw, w_up, w_conv, b_conv, w_down, nf, *, width, final_norm):
    b, t, d = x.shape
    f = w_down.shape[0]
    tt = min(FFN_TILE, t)
    nt = t // tt
    vertical = width < t
    ck = f // FFN_CHANNEL_BLOCKS
    nc = FFN_CHANNEL_BLOCKS
    wc = w_conv.reshape(9, 2 * f)
    bc = b_conv.reshape(1, 2 * f)
    hb = tt // width if vertical else 1
    nhb = t // width if vertical else 1

    if mod.shape[0] == 1:
        mspec = pl.BlockSpec((1, 1, mod.shape[2]), lambda bi, i, c: (0, 0, 0))
    else:
        mspec = pl.BlockSpec((1, 1, mod.shape[2]), lambda bi, i, c: (bi, 0, 0))
    tok = pl.BlockSpec((1, tt, d), lambda bi, i, c: (bi, i, 0))
    xs = [x]
    x_specs = [tok]
    if vertical:
        xs += [x, x]
        x_specs += [
            pl.BlockSpec((1, width, d), lambda bi, i, c: (bi, jnp.maximum(i * hb - 1, 0), 0)),
            pl.BlockSpec((1, width, d), lambda bi, i, c: (bi, jnp.minimum((i + 1) * hb, nhb - 1), 0)),
        ]
    in_specs = x_specs + [
        mspec,
        pl.BlockSpec(nw.shape, lambda bi, i, c: (0, 0)),
        pl.BlockSpec((d, ck), lambda bi, i, c: (0, c)),
        pl.BlockSpec((d, ck), lambda bi, i, c: (0, c + nc)),
        pl.BlockSpec((9, ck), lambda bi, i, c: (0, c)),
        pl.BlockSpec((9, ck), lambda bi, i, c: (0, c + nc)),
        pl.BlockSpec((1, ck), lambda bi, i, c: (0, c)),
        pl.BlockSpec((1, ck), lambda bi, i, c: (0, c + nc)),
        pl.BlockSpec((ck, d), lambda bi, i, c: (c, 0)),
        pl.BlockSpec(nf.shape, lambda bi, i, c: (0, 0)),
    ]
    rows = tt + 2 * width if vertical else tt
    kern = functools.partial(_ffn_kernel, width=width, vertical=vertical, final_norm=final_norm)
    return pl.pallas_call(
        kern,
        grid=(b, nt, nc),
        in_specs=in_specs,
        out_specs=tok,
        out_shape=jax.ShapeDtypeStruct((b, t, d), F32),
        scratch_shapes=[pltpu.VMEM((rows, d), BF16), pltpu.VMEM((tt, d), F32)],
        compiler_params=_params("arbitrary", "arbitrary", "arbitrary"),
        name="conv_ffn",
    )(*xs, mod, nw, w_up, w_up, wc, wc, bc, bc, w_down, nf)


def _sgu_kernel(x_ref, mod_ref, nw_ref, win_ref, lnw_ref, lnb_ref, ws_ref, bs_ref, wout_ref, o_ref):
    d = x_ref.shape[2]
    tt = x_ref.shape[1]
    e = wout_ref.shape[0]
    groups = ws_ref.shape[0]
    gw = e // groups
    shift = mod_ref[0, :, 0:d]
    scale = mod_ref[0, :, d:2 * d]
    gate = mod_ref[0, :, 2 * d:3 * d]
    chunk_rows = [slice(p * SGU_CHUNK, (p + 1) * SGU_CHUNK) for p in range(tt // SGU_CHUNK)]
    xs = [x_ref[0, rows, :] for rows in chunk_rows]
    zs = [_dot(_norm_mod(x, nw_ref[...], shift, scale), win_ref[...]) for x in xs]
    for rows, x, z in zip(chunk_rows, xs, zs):
        z = 0.5 * z * (1.0 + lax.erf(z * (1.0 / math.sqrt(2.0))))
        u = z[:, :e]
        v = z[:, e:]
        mu = jnp.mean(v, axis=-1, keepdims=True)
        vc = v - mu
        var = jnp.mean(vc * vc, axis=-1, keepdims=True)
        vn = (vc * lax.rsqrt(var + LN_EPS) * lnw_ref[...] + lnb_ref[...]).astype(BF16)
        parts = [jnp.dot(ws_ref[gi], vn[:, gi * gw:(gi + 1) * gw], preferred_element_type=F32)
                 for gi in range(groups)]
        vm = jnp.concatenate(parts, axis=1) + bs_ref[...]
        out = _dot(u * vm, wout_ref[...])
        o_ref[0, rows, :] = x + gate * out


def _sgu(x, mod, nw, w_in, ln_w, ln_b, w_s, b_full, w_out):
    b, t, d = x.shape
    tt = TOKEN_TILE
    tok = pl.BlockSpec((1, tt, d), lambda bi, i: (bi, i, 0))
    consts = (nw, w_in, ln_w, ln_b, w_s, b_full, w_out)
    return pl.pallas_call(
        _sgu_kernel,
        grid=(b, t // tt),
        in_specs=[tok, _mod_spec(mod)] + [_const_spec(c) for c in consts],
        out_specs=tok,
        out_shape=jax.ShapeDtypeStruct((b, t, d), F32),
        compiler_params=_params("arbitrary", "arbitrary"),
        name="sgu",
    )(x, mod, *consts)


def _lane_param(p, heads):
    m = p.reshape(heads // 2, 2, HEAD_DIM).transpose(2, 1, 0)
    m = jnp.broadcast_to(m[:, :, None, None, :], (HEAD_DIM, 2, 2, SCAN_BATCH, heads // 2))
    return m.reshape(HEAD_DIM, LANES)


def _state_to_scan(sf, sb):
    b, h = sf.shape[:2]
    g = b // SCAN_BATCH
    s = jnp.stack([sf, sb], axis=0).reshape(2, g, SCAN_BATCH, h // 2, 2, HEAD_DIM, HEAD_DIM)
    return s.transpose(1, 6, 5, 4, 0, 2, 3).reshape(g, HEAD_DIM, HEAD_DIM, LANES)


def _state_from_scan(s, heads):
    g = s.shape[0]
    s = s.reshape(g, HEAD_DIM, HEAD_DIM, 2, 2, SCAN_BATCH, heads // 2)
    s = s.transpose(4, 0, 5, 6, 3, 2, 1).reshape(2, g * SCAN_BATCH, heads, HEAD_DIM, HEAD_DIM)
    return s[0], s[1]


def kernel(x_prompt, x_sample, state_ctx_fwd, state_ctx_bwd, c, c_ctx, ada_w, ada_b, norm_mix, norm_ffn, ffn_up, ffn_conv, ffn_conv_b, ffn_down, norm_final, rw_mu, rw_wr, rw_wk, rw_wv, rw_wo, rw_w0, rw_w1, rw_w2, rw_a0, rw_a1, rw_a2, rw_g1, rw_g2, rw_kk, rw_ka, rw_rk, rw_lnx_w, rw_lnx_b, sg_in, sg_ln_w, sg_ln_b, sg_ws, sg_bs, sg_out):
    depth, d, _ = ada_w.shape
    heads = d // HEAD_DIM
    n_dec = c.shape[0]
    assert SCAN_BATCH * heads * 2 == LANES
    assert n_dec + 1 <= SUBLANES

    cond8 = jnp.concatenate(
        [c_ctx[None, :], c, jnp.zeros((SUBLANES - 1 - n_dec, d), F32)], axis=0)
    mod_all = _modulation(cond8, ada_w, ada_b)
    nf = norm_final.reshape(1, d)

    def run_stream(x, mod_rows, s0f, s0b, width, want_state):
        b, t, _ = x.shape
        new_f, new_b = [], []
        for i in range(depth):
            mod = mod_all[i, mod_rows[0]:mod_rows[1]][:, None, :]
            j = i // 2
            nw_mix = norm_mix[i].reshape(1, d)
            if i % 2 == 0:
                w1 = jnp.concatenate([rw_w1[j, 0], rw_w1[j, 1]], axis=1).astype(BF16)
                a1 = jnp.concatenate([rw_a1[j, 0], rw_a1[j, 1]], axis=1).astype(BF16)
                r, k, v, df, db, af, ab, g = _rw_proj(
                    x, mod, nw_mix, rw_mu[j],
                    rw_wr[j].astype(BF16), rw_wk[j].astype(BF16), rw_wv[j].astype(BF16),
                    w1, a1, rw_g1[j].astype(BF16),
                    rw_w2[j].astype(BF16), rw_a2[j].astype(BF16), rw_w0[j], rw_a0[j])
                s0 = _state_to_scan(s0f[:, j], s0b[:, j])
                ys, bvs, sfin = _scan(
                    r, k, v, df, db, af, ab, s0,
                    _lane_param(rw_kk[j], heads), _lane_param(rw_ka[j], heads),
                    _lane_param(rw_rk[j].reshape(-1), heads))
                lo, hi = _rw_post(ys, bvs, _lane_param(rw_lnx_w[j], heads),
                                  _lane_param(rw_lnx_b[j], heads), d)
                yn = jnp.concatenate([lo, hi], axis=1)
                x = _rw_out(x, yn, g, mod, rw_g2[j].astype(BF16), rw_wo[j].astype(BF16))
                if want_state:
                    sf, sb = _state_from_scan(sfin, heads)
                    new_f.append(sf)
                    new_b.append(sb)
            else:
                e = sg_out.shape[1]
                b_full = jnp.repeat(sg_bs[j].T, e // sg_bs.shape[1], axis=1)
                x = _sgu(x, mod, nw_mix, sg_in[j].astype(BF16),
                         sg_ln_w[j].reshape(1, e), sg_ln_b[j].reshape(1, e),
                         sg_ws[j].astype(BF16), b_full, sg_out[j].astype(BF16))
            x = _ffn(x, mod, norm_ffn[i].reshape(1, d), ffn_up[i].astype(BF16), ffn_conv[i],
                     ffn_conv_b[i], ffn_down[i].astype(BF16), nf,
                     width=width, final_norm=(i == depth - 1))
        if want_state:
            return x, jnp.stack(new_f, axis=1), jnp.stack(new_b, axis=1)
        return x, None, None

    bp = x_prompt.shape[0]
    n_rwkv = state_ctx_fwd.shape[1]
    zero_state = jnp.zeros((bp, n_rwkv, heads, HEAD_DIM, HEAD_DIM), F32)
    y_prompt, new_f, new_b = run_stream(x_prompt, (0, 1), zero_state, zero_state,
                                        x_prompt.shape[1], True)
    y_sample, _, _ = run_stream(x_sample, (1, 1 + n_dec), state_ctx_fwd, state_ctx_bwd,
                                GRID_W, False)
    return (y_prompt, y_sample, new_f, new_b)
```

```python
import functools
import math

import jax
import jax.numpy as jnp
from jax import lax
from jax.experimental import pallas as pl
from jax.experimental.pallas import tpu as pltpu

RMS_EPS = 1e-6
LN_EPS = 1e-5
GN_EPS = 64e-5
HEAD_DIM = 64
SGU_CHUNK = 128
GRID_W = 64

LANES = 128
SUBLANES = 8
VMEM_LIMIT = 56 * 1024 * 1024

TOKEN_TILE = 256
FFN_TILE = 512
FFN_CHANNEL_BLOCKS = 2
SCAN_BATCH = 4
SCAN_TCHUNK = 64
SCAN_PREP_AT = 52
SCAN_ACCS = 1
SCAN_SLOTS = 3
POST_TCHUNK = 64

BF16 = jnp.bfloat16
F32 = jnp.float32


def _params(*sem):
    return pltpu.CompilerParams(dimension_semantics=sem, vmem_limit_bytes=VMEM_LIMIT)


def _dot(a, b):
    return jnp.dot(a.astype(BF16), b, preferred_element_type=F32)


def _norm_mod(x, g, shift, scale):
    ms = jnp.mean(x * x, axis=-1, keepdims=True)
    return x * lax.rsqrt(ms + RMS_EPS) * g * (1.0 + scale) + shift


def _sigmoid(x):
    return 1.0 / (1.0 + jnp.exp(-x))


def _tree_sum(xs):
    while len(xs) > 1:
        xs = [xs[i] + xs[i + 1] for i in range(0, len(xs), 2)]
    return xs[0]


def _tile_major_shape(b, t, d):
    return (b, t // SUBLANES, d // LANES * SUBLANES, LANES)


def _put_tiles(ref, val):
    n = val.shape[0] // SUBLANES
    for p in range(val.shape[1] // LANES):
        ref[0, :, p * SUBLANES:(p + 1) * SUBLANES, :] = (
            val[:, p * LANES:(p + 1) * LANES].reshape(n, SUBLANES, LANES))


def _get_tiles(ref):
    _, n, rows, _ = ref.shape
    return jnp.concatenate(
        [ref[0, :, p * SUBLANES:(p + 1) * SUBLANES, :].reshape(n * SUBLANES, LANES)
         for p in range(rows // SUBLANES)], axis=1)


def _token_rows(u, pairs):
    return pl.ds(u, pairs, stride=SUBLANES)


def _mod_kernel(c_ref, w_ref, b_ref, o_ref):
    c = c_ref[...]
    sc = c * _sigmoid(c)
    o_ref[0] = jnp.dot(sc, w_ref[0], preferred_element_type=F32,
                       precision=lax.Precision.HIGHEST) + b_ref[0]


def _modulation(cond8, ada_w, ada_b):
    depth, d, n = ada_w.shape
    tn = n // 4
    return pl.pallas_call(
        _mod_kernel,
        grid=(depth, n // tn),
        in_specs=[
            pl.BlockSpec((SUBLANES, d), lambda l, j: (0, 0)),
            pl.BlockSpec((1, d, tn), lambda l, j: (l, 0, j)),
            pl.BlockSpec((1, 1, tn), lambda l, j: (l, 0, j)),
        ],
        out_specs=pl.BlockSpec((1, SUBLANES, tn), lambda l, j: (l, 0, j)),
        out_shape=jax.ShapeDtypeStruct((depth, SUBLANES, n), F32),
        compiler_params=_params("arbitrary", "arbitrary"),
        name="modulation",
    )(cond8, ada_w, ada_b.reshape(depth, 1, n))


def _mod_spec(mod):
    if mod.shape[0] == 1:
        return pl.BlockSpec((1, 1, mod.shape[2]), lambda b, i: (0, 0, 0))
    return pl.BlockSpec((1, 1, mod.shape[2]), lambda b, i: (b, 0, 0))


def _const_spec(a):
    nd = a.ndim
    return pl.BlockSpec(a.shape, lambda *_: (0,) * nd)


def _rw_proj_kernel(x_ref, xp_ref, xn_ref, mod_ref, nw_ref, mu_ref,
                    wr_ref, wk_ref, wv_ref, w1_ref, a1_ref, g1_ref,
                    w2_ref, a2_ref, w0_ref, a0_ref,
                    r_ref, k_ref, v_ref, df_ref, db_ref, af_ref, ab_ref, g_ref):
    i = pl.program_id(1)
    last = pl.num_programs(1) - 1
    d = x_ref.shape[2]
    tt = x_ref.shape[1]
    nw = nw_ref[...]
    shift = mod_ref[0, :, 0:d]
    scale = mod_ref[0, :, d:2 * d]
    h = _norm_mod(x_ref[0], nw, shift, scale)
    hp = _norm_mod(xp_ref[0], nw, shift, scale)[SUBLANES - 1:SUBLANES, :]
    hn = _norm_mod(xn_ref[0], nw, shift, scale)[0:1, :]
    hp = jnp.where(i == 0, 0.0, hp)
    hn = jnp.where(i == last, 0.0, hn)
    rows = lax.broadcasted_iota(jnp.int32, (tt, 1), 0)
    prev = jnp.where(rows == 0, hp, pltpu.roll(h, 1, axis=0))
    nxt = jnp.where(rows == tt - 1, hn, pltpu.roll(h, tt - 1, axis=0))
    dp = prev - h
    dn = nxt - h

    def shifted(idx):
        return h + dp * mu_ref[0, idx:idx + 1, :] + dn * mu_ref[1, idx:idx + 1, :]

    _put_tiles(r_ref, _dot(shifted(0), wr_ref[...]))
    _put_tiles(k_ref, _dot(shifted(2), wk_ref[...]))
    _put_tiles(v_ref, _dot(shifted(3), wv_ref[...]))
    g_ref[0] = _sigmoid(_dot(shifted(5), g1_ref[...]))
    lw = jnp.tanh(_dot(shifted(1), w1_ref[...]))
    la = _dot(shifted(4), a1_ref[...])
    lora = w2_ref.shape[1]
    decay_scale = math.exp(-0.5)
    for dirn, (d_ref, a_ref) in enumerate(((df_ref, af_ref), (db_ref, ab_ref))):
        w_raw = w0_ref[dirn:dirn + 1, :] + _dot(lw[:, dirn * lora:(dirn + 1) * lora], w2_ref[dirn])
        _put_tiles(d_ref, jnp.exp(-decay_scale * _sigmoid(w_raw)))
        a_raw = a0_ref[dirn:dirn + 1, :] + _dot(la[:, dirn * lora:(dirn + 1) * lora], a2_ref[dirn])
        _put_tiles(a_ref, _sigmoid(a_raw))


def _rw_proj(x, mod, nw, mu, wr, wk, wv, w1, a1, g1, w2, a2, w0, a0):
    b, t, d = x.shape
    tt = TOKEN_TILE
    nt = t // tt
    hb = tt // SUBLANES
    nhb = t // SUBLANES
    tok = pl.BlockSpec((1, tt, d), lambda bi, i: (bi, i, 0))
    prev = pl.BlockSpec((1, SUBLANES, d), lambda bi, i: (bi, jnp.maximum(i * hb - 1, 0), 0))
    nxt = pl.BlockSpec((1, SUBLANES, d), lambda bi, i: (bi, jnp.minimum((i + 1) * hb, nhb - 1), 0))
    consts = (nw, mu, wr, wk, wv, w1, a1, g1, w2, a2, w0, a0)
    gl = g1.shape[1]
    tiles = _tile_major_shape(b, t, d)
    tile_spec = pl.BlockSpec((1, tt // SUBLANES) + tiles[2:], lambda bi, i: (bi, i, 0, 0))
    out_shape = [jax.ShapeDtypeStruct(tiles, F32)] * 7 + [jax.ShapeDtypeStruct((b, t, gl), F32)]
    out_specs = [tile_spec] * 7 + [pl.BlockSpec((1, tt, gl), lambda bi, i: (bi, i, 0))]
    return pl.pallas_call(
        _rw_proj_kernel,
        grid=(b, nt),
        in_specs=[tok, prev, nxt, _mod_spec(mod)] + [_const_spec(c) for c in consts],
        out_specs=out_specs,
        out_shape=out_shape,
        compiler_params=_params("arbitrary", "arbitrary"),
        name="rw_proj",
    )(x, x, x, mod, *consts)


def _scan_kernel(rf_ref, rm_ref, kf_ref, km_ref, vf_ref, vm_ref, wf_ref, wm_ref, af_ref, am_ref,
                 s0_ref, kk_ref, ka_ref, rk_ref, y_ref, bv_ref, sfin_ref,
                 s_ref, ops_ref, sa_ref, stage_ref, cols_ref):
    c = pl.program_id(1)
    nb, ntb, pairs = rf_ref.shape[0], rf_ref.shape[1], rf_ref.shape[2] // SUBLANES
    tc = ntb * SUBLANES
    hd = s_ref.shape[0]

    @pl.when(c == 0)
    def _():
        s_ref[...] = s0_ref[0]

    operands = ((rf_ref, rm_ref), (kf_ref, km_ref), (vf_ref, vm_ref), (af_ref, am_ref),
                (wf_ref, wm_ref))

    def gather_rows(tb):
        half = tb % 2
        for op, (fwd_ref, bwd_ref) in enumerate(operands):
            for u in range(SUBLANES):
                for b in range(nb):
                    stage_ref[half, op, u, b * pairs:(b + 1) * pairs, :] = (
                        fwd_ref[b, tb, _token_rows(u, pairs), :])
                    stage_ref[half, op, u, (nb + b) * pairs:(nb + b + 1) * pairs, :] = (
                        bwd_ref[b, ntb - 1 - tb, _token_rows(SUBLANES - 1 - u, pairs), :])

    def transpose_rows(s):
        half = (s // SUBLANES) % 2
        u = s % SUBLANES
        for op in range(len(operands)):
            cols = stage_ref[half, op, u].T
            cols_ref[op] = jnp.concatenate([cols[0:hd], cols[hd:2 * hd]], axis=1)

    def prepare(s, slot):
        r = cols_ref[0]
        kraw = cols_ref[1]
        v = cols_ref[2]
        a = cols_ref[3]
        kkv = kraw * kk_ref[...]
        nrm = jnp.sqrt(jnp.sum(kkv * kkv, axis=0, keepdims=True))
        kk = kkv / jnp.maximum(nrm, 1e-12)
        kd = kraw * (1.0 + (a - 1.0) * ka_ref[...])
        p_prev = ops_ref[(s + SCAN_SLOTS - 1) % SCAN_SLOTS, 5]
        p = p_prev * cols_ref[4]
        inv_p = 1.0 / p
        ops_ref[slot, 0] = -kk * p_prev
        ops_ref[slot, 1] = kk * a * inv_p
        ops_ref[slot, 2] = kd * inv_p
        ops_ref[slot, 3] = r * p
        ops_ref[slot, 4] = v
        ops_ref[slot, 5] = p
        bv_ref[0, s] = v * jnp.sum(r * kd * rk_ref[...], axis=0, keepdims=True)

    def accumulate(acc, j, term):
        n = j % SCAN_ACCS
        acc[n] = term if acc[n] is None else acc[n] + term

    def first_sa(slot):
        acc = [None] * SCAN_ACCS
        for j in range(hd):
            accumulate(acc, j, s_ref[j] * ops_ref[slot, 0, j:j + 1, :])
        sa_ref[...] = _tree_sum(acc)

    def advance(s, slot, nxt, midway):
        sa = sa_ref[...]
        vr = ops_ref[slot, 4]
        acc_y = [None] * SCAN_ACCS
        acc_a = [None] * SCAN_ACCS
        for j in range(hd):
            if j == SCAN_PREP_AT:
                midway()
            sn = s_ref[j] + sa * ops_ref[slot, 1, j:j + 1, :] + vr * ops_ref[slot, 2, j:j + 1, :]
            s_ref[j] = sn
            accumulate(acc_y, j, sn * ops_ref[slot, 3, j:j + 1, :])
            accumulate(acc_a, j, sn * ops_ref[nxt, 0, j:j + 1, :])
        y_ref[0, s] = _tree_sum(acc_y)
        sa_ref[...] = _tree_sum(acc_a)

    ops_ref[SCAN_SLOTS - 1, 5] = jnp.ones((hd, LANES), F32)
    gather_rows(0)
    for s in range(2):
        transpose_rows(s)
        prepare(s, s)
    first_sa(0)

    def step(s, carry):
        ahead = jnp.minimum(s + 2, tc - 1)
        transpose_rows(ahead)
        advance(s, s % SCAN_SLOTS, (s + 1) % SCAN_SLOTS,
                functools.partial(prepare, ahead, (s + 2) % SCAN_SLOTS))
        return carry

    def token_block(tb, carry):
        gather_rows(jnp.minimum(tb + 1, ntb - 1))
        return lax.fori_loop(tb * SUBLANES, (tb + 1) * SUBLANES, step, carry)

    lax.fori_loop(0, ntb, token_block, 0)

    for j in range(hd):
        s_ref[j] = s_ref[j] * ops_ref[(tc - 1) % SCAN_SLOTS, 5, j:j + 1, :]

    @pl.when(c == pl.num_programs(1) - 1)
    def _():
        sfin_ref[0] = s_ref[...]


def _scan(r, k, v, wf, wb, af, ab, s0, kk, ka, rk):
    b, tblocks, _, _ = r.shape
    t = tblocks * SUBLANES
    hd = HEAD_DIM
    g = b // SCAN_BATCH
    tc = SCAN_TCHUNK
    nt = t // tc
    blk = (SCAN_BATCH, tc // SUBLANES) + r.shape[2:]
    fwd = pl.BlockSpec(blk, lambda gi, c: (gi, c, 0, 0))
    bwd = pl.BlockSpec(blk, lambda gi, c: (gi, nt - 1 - c, 0, 0))
    step = pl.BlockSpec((1, tc, hd, LANES), lambda gi, c: (gi, c, 0, 0))
    state = pl.BlockSpec((1, hd, hd, LANES), lambda gi, c: (gi, 0, 0, 0))
    steps = jax.ShapeDtypeStruct((g, t, hd, LANES), F32)
    return pl.pallas_call(
        _scan_kernel,
        grid=(g, nt),
        in_specs=[fwd, bwd] * 5 + [state, _const_spec(kk), _const_spec(ka), _const_spec(rk)],
        out_specs=[step, step, state],
        out_shape=[steps, steps, jax.ShapeDtypeStruct((g, hd, hd, LANES), F32)],
        scratch_shapes=[pltpu.VMEM((hd, hd, LANES), F32),
                        pltpu.VMEM((SCAN_SLOTS, 6, hd, LANES), F32),
                        pltpu.VMEM((hd, LANES), F32),
                        pltpu.VMEM((2, 5, SUBLANES, 2 * SCAN_BATCH * r.shape[2] // SUBLANES, LANES), F32),
                        pltpu.VMEM((5, hd, LANES), F32)],
        compiler_params=_params("arbitrary", "arbitrary"),
        name="wkv_scan",
    )(r, r, k, k, v, v, wf, wb, af, ab, s0, kk, ka, rk)


def _rw_post_kernel(y_ref, ym_ref, bv_ref, bvm_ref, lw_ref, lb_ref, lo_ref, hi_ref):
    tp = y_ref.shape[1]
    hd = y_ref.shape[2]
    nb, ntb, pairs = lo_ref.shape[0], lo_ref.shape[1], lo_ref.shape[2] // SUBLANES
    quarter = LANES // 4
    lane = lax.broadcasted_iota(jnp.int32, (hd, LANES), 1)
    fwd_lane = (lane % (2 * quarter)) < quarter

    def swap_dirs(x):
        return jnp.where(fwd_lane, pltpu.roll(x, LANES - quarter, axis=1), pltpu.roll(x, quarter, axis=1))

    def token_block(tb, carry):
        ts = [tb * SUBLANES + u for u in range(SUBLANES)]
        ys = [y_ref[0, t] + swap_dirs(ym_ref[0, tp - 1 - t]) for t in ts]
        bonuses = [bv_ref[0, t] + swap_dirs(bvm_ref[0, tp - 1 - t]) for t in ts]
        outs = []
        for y, bonus in zip(ys, bonuses):
            mean = jnp.mean(y, axis=0, keepdims=True)
            yc = y - mean
            var = jnp.mean(yc * yc, axis=0, keepdims=True)
            yn = yc * lax.rsqrt(var + GN_EPS) * lw_ref[...] + lb_ref[...]
            outs.append(yn + bonus)
        cols = [o.T for o in outs]
        tiles = [jnp.concatenate([c[0:hd], c[hd:2 * hd]], axis=1) for c in cols]
        for u, rows in enumerate(tiles):
            for b in range(nb):
                lo_ref[b, tb, _token_rows(u, pairs), :] = rows[b * pairs:(b + 1) * pairs]
                hi_ref[b, ntb - 1 - tb, _token_rows(SUBLANES - 1 - u, pairs), :] = (
                    rows[(nb + b) * pairs:(nb + b + 1) * pairs])
        return carry

    lax.fori_loop(0, ntb, token_block, 0)


def _rw_post(y, bv, lw, lb, d):
    g, t, hd, _ = y.shape
    tp = POST_TCHUNK
    nb = t // tp
    cur = pl.BlockSpec((1, tp, hd, LANES), lambda gi, c: (gi, c, 0, 0))
    mir = pl.BlockSpec((1, tp, hd, LANES), lambda gi, c: (gi, nb - 1 - c, 0, 0))
    tiles = _tile_major_shape(g * SCAN_BATCH, t // 2, d)
    blk = (SCAN_BATCH, tp // SUBLANES) + tiles[2:]
    lo = pl.BlockSpec(blk, lambda gi, c: (gi, c, 0, 0))
    hi = pl.BlockSpec(blk, lambda gi, c: (gi, nb // 2 - 1 - c, 0, 0))
    half_tokens = jax.ShapeDtypeStruct(tiles, F32)
    return pl.pallas_call(
        _rw_post_kernel,
        grid=(g, nb // 2),
        in_specs=[cur, mir, cur, mir, _const_spec(lw), _const_spec(lb)],
        out_specs=[lo, hi],
        out_shape=[half_tokens, half_tokens],
        compiler_params=_params("arbitrary", "arbitrary"),
        name="rw_post",
    )(y, y, bv, bv, lw, lb)


def _rw_out_kernel(x_ref, yn_ref, g_ref, mod_ref, g2_ref, wo_ref, o_ref):
    d = x_ref.shape[2]
    gate = mod_ref[0, :, 2 * d:3 * d]
    g = _dot(g_ref[0], g2_ref[...])
    out = _dot(_get_tiles(yn_ref) * g, wo_ref[...])
    o_ref[0] = x_ref[0] + gate * out


def _rw_out(x, yn, g, mod, g2, wo):
    b, t, d = x.shape
    tt = TOKEN_TILE
    tok = pl.BlockSpec((1, tt, d), lambda bi, i: (bi, i, 0))
    tiles = pl.BlockSpec((1, tt // SUBLANES) + yn.shape[2:], lambda bi, i: (bi, i, 0, 0))
    gspec = pl.BlockSpec((1, tt, g.shape[2]), lambda bi, i: (bi, i, 0))
    return pl.pallas_call(
        _rw_out_kernel,
        grid=(b, t // tt),
        in_specs=[tok, tiles, gspec, _mod_spec(mod), _const_spec(g2), _const_spec(wo)],
        out_specs=tok,
        out_shape=jax.ShapeDtypeStruct((b, t, d), F32),
        compiler_params=_params("arbitrary", "arbitrary"),
        name="rw_out",
    )(x, yn, g, mod, g2, wo)


def _conv_taps(cur, up, down, w_ref, width, vertical):
    n = cur.shape[0]
    col = lax.broadcasted_iota(jnp.int32, (n, 1), 0) % width
    if vertical:
        ext = jnp.concatenate([up, cur, down], axis=0)
        slabs = [(0, ext[0:n]), (1, cur), (2, ext[2 * width:2 * width + n])]
    else:
        slabs = [(1, cur)]
    z = []
    for dc in range(3):
        acc = None
        for dr, rows in slabs:
            term = rows * w_ref[3 * dr + dc:3 * dr + dc + 1, :]
            acc = term if acc is None else acc + term
        z.append(acc)
    left = jnp.where(col != 0, pltpu.roll(z[0], 1, axis=0), 0.0)
    right = jnp.where(col != width - 1, pltpu.roll(z[2], n - 1, axis=0), 0.0)
    return left + z[1] + right


def _ffn_kernel(*refs, width, vertical, final_norm):
    if vertical:
        (x_ref, xup_ref, xdn_ref, mod_ref, nw_ref, wuv_ref, wug_ref,
         wcv_ref, wcg_ref, bv_ref, bg_ref, wd_ref, nf_ref, o_ref, h_ref, acc_ref) = refs
    else:
        (x_ref, mod_ref, nw_ref, wuv_ref, wug_ref,
         wcv_ref, wcg_ref, bv_ref, bg_ref, wd_ref, nf_ref, o_ref, h_ref, acc_ref) = refs
    i = pl.program_id(1)
    c = pl.program_id(2)
    tt, d = x_ref.shape[1], x_ref.shape[2]
    pad = width if vertical else 0

    @pl.when(c == 0)
    def _():
        shift = mod_ref[0, :, 3 * d:4 * d]
        scale = mod_ref[0, :, 4 * d:5 * d]
        h_ref[pad:pad + tt, :] = _norm_mod(x_ref[0], nw_ref[...], shift, scale).astype(BF16)
        if vertical:
            h_ref[0:pad, :] = _norm_mod(xup_ref[0], nw_ref[...], shift, scale).astype(BF16)
            h_ref[pad + tt:, :] = _norm_mod(xdn_ref[0], nw_ref[...], shift, scale).astype(BF16)

    top = i == 0
    bottom = i == pl.num_programs(1) - 1

    def conv(w_up_ref, w_conv_ref):
        up = jnp.dot(h_ref[...], w_up_ref[...], preferred_element_type=F32)
        if not vertical:
            return _conv_taps(up, None, None, w_conv_ref, width, vertical)
        above = jnp.where(top, 0.0, up[0:pad])
        below = jnp.where(bottom, 0.0, up[pad + tt:])
        return _conv_taps(up[pad:pad + tt], above, below, w_conv_ref, width, vertical)

    val = conv(wuv_ref, wcv_ref) + bv_ref[...]
    gat = conv(wug_ref, wcg_ref) + bg_ref[...]
    act = gat * _sigmoid(gat) * val
    part = _dot(act, wd_ref[...])

    @pl.when(c == 0)
    def _():
        acc_ref[...] = part

    @pl.when(c != 0)
    def _():
        acc_ref[...] += part

    @pl.when(c == pl.num_programs(2) - 1)
    def _():
        gate = mod_ref[0, :, 5 * d:6 * d]
        y = x_ref[0] + gate * acc_ref[...]
        if final_norm:
            ms = jnp.mean(y * y, axis=-1, keepdims=True)
            y = y * lax.rsqrt(ms + RMS_EPS) * nf_ref[...]
        o_ref[0] = y


def _ffn(x, mod, nw, w_up, w_conv, b_conv, w_down, nf, *, width, final_norm):
    b, t, d = x.shape
    f = w_down.shape[0]
    tt = min(FFN_TILE, t)
    nt = t // tt
    vertical = width < t
    ck = f // FFN_CHANNEL_BLOCKS
    nc = FFN_CHANNEL_BLOCKS
    wc = w_conv.reshape(9, 2 * f)
    bc = b_conv.reshape(1, 2 * f)
    hb = tt // width if vertical else 1
    nhb = t // width if vertical else 1

    if mod.shape[0] == 1:
        mspec = pl.BlockSpec((1, 1, mod.shape[2]), lambda bi, i, c: (0, 0, 0))
    else:
        mspec = pl.BlockSpec((1, 1, mod.shape[2]), lambda bi, i, c: (bi, 0, 0))
    tok = pl.BlockSpec((1, tt, d), lambda bi, i, c: (bi, i, 0))
    xs = [x]
    x_specs = [tok]
    if vertical:
        xs += [x, x]
        x_specs += [
            pl.BlockSpec((1, width, d), lambda bi, i, c: (bi, jnp.maximum(i * hb - 1, 0), 0)),
            pl.BlockSpec((1, width, d), lambda bi, i, c: (bi, jnp.minimum((i + 1) * hb, nhb - 1), 0)),
        ]
    in_specs = x_specs + [
        mspec,
        pl.BlockSpec(nw.shape, lambda bi, i, c: (0, 0)),
        pl.BlockSpec((d, ck), lambda bi, i, c: (0, c)),
        pl.BlockSpec((d, ck), lambda bi, i, c: (0, c + nc)),
        pl.BlockSpec((9, ck), lambda bi, i, c: (0, c)),
        pl.BlockSpec((9, ck), lambda bi, i, c: (0, c + nc)),
        pl.BlockSpec((1, ck), lambda bi, i, c: (0, c)),
        pl.BlockSpec((1, ck), lambda bi, i, c: (0, c + nc)),
        pl.BlockSpec((ck, d), lambda bi, i, c: (c, 0)),
        pl.BlockSpec(nf.shape, lambda bi, i, c: (0, 0)),
    ]
    rows = tt + 2 * width if vertical else tt
    kern = functools.partial(_ffn_kernel, width=width, vertical=vertical, final_norm=final_norm)
    return pl.pallas_call(
        kern,
        grid=(b, nt, nc),
        in_specs=in_specs,
        out_specs=tok,
        out_shape=jax.ShapeDtypeStruct((b, t, d), F32),
        scratch_shapes=[pltpu.VMEM((rows, d), BF16), pltpu.VMEM((tt, d), F32)],
        compiler_params=_params("arbitrary", "arbitrary", "arbitrary"),
        name="conv_ffn",
    )(*xs, mod, nw, w_up, w_up, wc, wc, bc, bc, w_down, nf)


def _sgu_kernel(x_ref, mod_ref, nw_ref, win_ref, lnw_ref, lnb_ref, ws_ref, bs_ref, wout_ref, o_ref):
    d = x_ref.shape[2]
    tt = x_ref.shape[1]
    e = wout_ref.shape[0]
    groups = ws_ref.shape[0]
    gw = e // groups
    shift = mod_ref[0, :, 0:d]
    scale = mod_ref[0, :, d:2 * d]
    gate = mod_ref[0, :, 2 * d:3 * d]
    chunk_rows = [slice(p * SGU_CHUNK, (p + 1) * SGU_CHUNK) for p in range(tt // SGU_CHUNK)]
    xs = [x_ref[0, rows, :] for rows in chunk_rows]
    zs = [_dot(_norm_mod(x, nw_ref[...], shift, scale), win_ref[...]) for x in xs]
    for rows, x, z in zip(chunk_rows, xs, zs):
        z = 0.5 * z * (1.0 + lax.erf(z * (1.0 / math.sqrt(2.0))))
        u = z[:, :e]
        v = z[:, e:]
        mu = jnp.mean(v, axis=-1, keepdims=True)
        vc = v - mu
        var = jnp.mean(vc * vc, axis=-1, keepdims=True)
        vn = (vc * lax.rsqrt(var + LN_EPS) * lnw_ref[...] + lnb_ref[...]).astype(BF16)
        parts = [jnp.dot(ws_ref[gi], vn[:, gi * gw:(gi + 1) * gw], preferred_element_type=F32)
                 for gi in range(groups)]
        vm = jnp.concatenate(parts, axis=1) + bs_ref[...]
        out = _dot(u * vm, wout_ref[...])
        o_ref[0, rows, :] = x + gate * out


def _sgu(x, mod, nw, w_in, ln_w, ln_b, w_s, b_full, w_out):
    b, t, d = x.shape
    tt = TOKEN_TILE
    tok = pl.BlockSpec((1, tt, d), lambda bi, i: (bi, i, 0))
    consts = (nw, w_in, ln_w, ln_b, w_s, b_full, w_out)
    return pl.pallas_call(
        _sgu_kernel,
        grid=(b, t // tt),
        in_specs=[tok, _mod_spec(mod)] + [_const_spec(c) for c in consts],
        out_specs=tok,
        out_shape=jax.ShapeDtypeStruct((b, t, d), F32),
        compiler_params=_params("arbitrary", "arbitrary"),
        name="sgu",
    )(x, mod, *consts)


def _lane_param(p, heads):
    m = p.reshape(heads // 2, 2, HEAD_DIM).transpose(2, 1, 0)
    m = jnp.broadcast_to(m[:, :, None, None, :], (HEAD_DIM, 2, 2, SCAN_BATCH, heads // 2))
    return m.reshape(HEAD_DIM, LANES)


def _state_to_scan(sf, sb):
    b, h = sf.shape[:2]
    g = b // SCAN_BATCH
    s = jnp.stack([sf, sb], axis=0).reshape(2, g, SCAN_BATCH, h // 2, 2, HEAD_DIM, HEAD_DIM)
    return s.transpose(1, 6, 5, 4, 0, 2, 3).reshape(g, HEAD_DIM, HEAD_DIM, LANES)


def _state_from_scan(s, heads):
    g = s.shape[0]
    s = s.reshape(g, HEAD_DIM, HEAD_DIM, 2, 2, SCAN_BATCH, heads // 2)
    s = s.transpose(4, 0, 5, 6, 3, 2, 1).reshape(2, g * SCAN_BATCH, heads, HEAD_DIM, HEAD_DIM)
    return s[0], s[1]


def kernel(x_prompt, x_sample, state_ctx_fwd, state_ctx_bwd, c, c_ctx, ada_w, ada_b, norm_mix, norm_ffn, ffn_up, ffn_conv, ffn_conv_b, ffn_down, norm_final, rw_mu, rw_wr, rw_wk, rw_wv, rw_wo, rw_w0, rw_w1, rw_w2, rw_a0, rw_a1, rw_a2, rw_g1, rw_g2, rw_kk, rw_ka, rw_rk, rw_lnx_w, rw_lnx_b, sg_in, sg_ln_w, sg_ln_b, sg_ws, sg_bs, sg_out):
    depth, d, _ = ada_w.shape
    heads = d // HEAD_DIM
    n_dec = c.shape[0]
    assert SCAN_BATCH * heads * 2 == LANES
    assert n_dec + 1 <= SUBLANES

    cond8 = jnp.concatenate(
        [c_ctx[None, :], c, jnp.zeros((SUBLANES - 1 - n_dec, d), F32)], axis=0)
    mod_all = _modulation(cond8, ada_w, ada_b)
    nf = norm_final.reshape(1, d)

    def run_stream(x, mod_rows, s0f, s0b, width, want_state):
        b, t, _ = x.shape
        new_f, new_b = [], []
        for i in range(depth):
            mod = mod_all[i, mod_rows[0]:mod_rows[1]][:, None, :]
            j = i // 2
            nw_mix = norm_mix[i].reshape(1, d)
            if i % 2 == 0:
                w1 = jnp.concatenate([rw_w1[j, 0], rw_w1[j, 1]], axis=1).astype(BF16)
                a1 = jnp.concatenate([rw_a1[j, 0], rw_a1[j, 1]], axis=1).astype(BF16)
                r, k, v, df, db, af, ab, g = _rw_proj(
                    x, mod, nw_mix, rw_mu[j],
                    rw_wr[j].astype(BF16), rw_wk[j].astype(BF16), rw_wv[j].astype(BF16),
                    w1, a1, rw_g1[j].astype(BF16),
                    rw_w2[j].astype(BF16), rw_a2[j].astype(BF16), rw_w0[j], rw_a0[j])
                s0 = _state_to_scan(s0f[:, j], s0b[:, j])
                ys, bvs, sfin = _scan(
                    r, k, v, df, db, af, ab, s0,
                    _lane_param(rw_kk[j], heads), _lane_param(rw_ka[j], heads),
                    _lane_param(rw_rk[j].reshape(-1), heads))
                lo, hi = _rw_post(ys, bvs, _lane_param(rw_lnx_w[j], heads),
                                  _lane_param(rw_lnx_b[j], heads), d)
                yn = jnp.concatenate([lo, hi], axis=1)
                x = _rw_out(x, yn, g, mod, rw_g2[j].astype(BF16), rw_wo[j].astype(BF16))
                if want_state:
                    sf, sb = _state_from_scan(sfin, heads)
                    new_f.append(sf)
                    new_b.append(sb)
            else:
                e = sg_out.shape[1]
                b_full = jnp.repeat(sg_bs[j].T, e // sg_bs.shape[1], axis=1)
                x = _sgu(x, mod, nw_mix, sg_in[j].astype(BF16),
                         sg_ln_w[j].reshape(1, e), sg_ln_b[j].reshape(1, e),
                         sg_ws[j].astype(BF16), b_full, sg_out[j].astype(BF16))
            x = _ffn(x, mod, norm_ffn[i].reshape(1, d), ffn_up[i].astype(BF16), ffn_conv[i],
                     ffn_conv_b[i], ffn_down[i].astype(BF16), nf,
                     width=width, final_norm=(i == depth - 1))
        if want_state:
            return x, jnp.stack(new_f, axis=1), jnp.stack(new_b, axis=1)
        return x, None, None

    bp = x_prompt.shape[0]
    n_rwkv = state_ctx_fwd.shape[1]
    zero_state = jnp.zeros((bp, n_rwkv, heads, HEAD_DIM, HEAD_DIM), F32)
    y_prompt, new_f, new_b = run_stream(x_prompt, (0, 1), zero_state, zero_state,
                                        x_prompt.shape[1], True)
    y_sample, _, _ = run_stream(x_sample, (1, 1 + n_dec), state_ctx_fwd, state_ctx_bwd,
                                GRID_W, False)
    return (y_prompt, y_sample, new_f, new_b)
```

```python
import functools
import math

import jax
import jax.numpy as jnp
from jax import lax
from jax.experimental import pallas as pl
from jax.experimental.pallas import tpu as pltpu

RMS_EPS = 1e-6
LN_EPS = 1e-5
GN_EPS = 64e-5
HEAD_DIM = 64
SGU_CHUNK = 128
GRID_W = 64

LANES = 128
SUBLANES = 8
VMEM_LIMIT = 56 * 1024 * 1024

TOKEN_TILE = 256
RW_OUT_TILE = 512
FFN_TILE = 512
FFN_CHANNEL_BLOCKS = 2
SCAN_BATCH = 4
SCAN_TCHUNK = 64
SCAN_PREP_AT = 52
SCAN_ACCS = 1
SCAN_SLOTS = 3
POST_TCHUNK = 64

BF16 = jnp.bfloat16
F32 = jnp.float32


def _params(*sem):
    return pltpu.CompilerParams(dimension_semantics=sem, vmem_limit_bytes=VMEM_LIMIT)


def _dot(a, b):
    return jnp.dot(a.astype(BF16), b, preferred_element_type=F32)


def _norm_mod(x, g, shift, scale):
    ms = jnp.mean(x * x, axis=-1, keepdims=True)
    return x * lax.rsqrt(ms + RMS_EPS) * g * (1.0 + scale) + shift


def _sigmoid(x):
    return 1.0 / (1.0 + jnp.exp(-x))


def _tree_sum(xs):
    while len(xs) > 1:
        xs = [xs[i] + xs[i + 1] for i in range(0, len(xs), 2)]
    return xs[0]


def _tile_major_shape(b, t, d):
    return (b, t // SUBLANES, d // LANES * SUBLANES, LANES)


def _put_tiles(ref, val):
    n = val.shape[0] // SUBLANES
    for p in range(val.shape[1] // LANES):
        ref[0, :, p * SUBLANES:(p + 1) * SUBLANES, :] = (
            val[:, p * LANES:(p + 1) * LANES].reshape(n, SUBLANES, LANES))


def _get_tiles(ref):
    _, n, rows, _ = ref.shape
    return jnp.concatenate(
        [ref[0, :, p * SUBLANES:(p + 1) * SUBLANES, :].reshape(n * SUBLANES, LANES)
         for p in range(rows // SUBLANES)], axis=1)


def _token_rows(u, pairs):
    return pl.ds(u, pairs, stride=SUBLANES)


def _mod_kernel(c_ref, w_ref, b_ref, o_ref):
    c = c_ref[...]
    sc = c * _sigmoid(c)
    o_ref[0] = jnp.dot(sc, w_ref[0], preferred_element_type=F32,
                       precision=lax.Precision.HIGHEST) + b_ref[0]


def _modulation(cond8, ada_w, ada_b):
    depth, d, n = ada_w.shape
    tn = n // 4
    return pl.pallas_call(
        _mod_kernel,
        grid=(depth, n // tn),
        in_specs=[
            pl.BlockSpec((SUBLANES, d), lambda l, j: (0, 0)),
            pl.BlockSpec((1, d, tn), lambda l, j: (l, 0, j)),
            pl.BlockSpec((1, 1, tn), lambda l, j: (l, 0, j)),
        ],
        out_specs=pl.BlockSpec((1, SUBLANES, tn), lambda l, j: (l, 0, j)),
        out_shape=jax.ShapeDtypeStruct((depth, SUBLANES, n), F32),
        compiler_params=_params("arbitrary", "arbitrary"),
        name="modulation",
    )(cond8, ada_w, ada_b.reshape(depth, 1, n))


def _mod_spec(mod):
    if mod.shape[0] == 1:
        return pl.BlockSpec((1, 1, mod.shape[2]), lambda b, i: (0, 0, 0))
    return pl.BlockSpec((1, 1, mod.shape[2]), lambda b, i: (b, 0, 0))


def _const_spec(a):
    nd = a.ndim
    return pl.BlockSpec(a.shape, lambda *_: (0,) * nd)


def _rw_proj_kernel(x_ref, xp_ref, xn_ref, mod_ref, nw_ref, mu_ref,
                    wr_ref, wk_ref, wv_ref, w1_ref, a1_ref, g1_ref,
                    w2_ref, a2_ref, w0_ref, a0_ref,
                    r_ref, k_ref, v_ref, df_ref, db_ref, af_ref, ab_ref, g_ref):
    i = pl.program_id(1)
    last = pl.num_programs(1) - 1
    d = x_ref.shape[2]
    tt = x_ref.shape[1]
    nw = nw_ref[...]
    shift = mod_ref[0, :, 0:d]
    scale = mod_ref[0, :, d:2 * d]
    h = _norm_mod(x_ref[0], nw, shift, scale)
    hp = _norm_mod(xp_ref[0], nw, shift, scale)[SUBLANES - 1:SUBLANES, :]
    hn = _norm_mod(xn_ref[0], nw, shift, scale)[0:1, :]
    hp = jnp.where(i == 0, 0.0, hp)
    hn = jnp.where(i == last, 0.0, hn)
    rows = lax.broadcasted_iota(jnp.int32, (tt, 1), 0)
    prev = jnp.where(rows == 0, hp, pltpu.roll(h, 1, axis=0))
    nxt = jnp.where(rows == tt - 1, hn, pltpu.roll(h, tt - 1, axis=0))
    dp = prev - h
    dn = nxt - h

    def shifted(idx):
        return h + dp * mu_ref[0, idx:idx + 1, :] + dn * mu_ref[1, idx:idx + 1, :]

    _put_tiles(r_ref, _dot(shifted(0), wr_ref[...]))
    _put_tiles(k_ref, _dot(shifted(2), wk_ref[...]))
    _put_tiles(v_ref, _dot(shifted(3), wv_ref[...]))
    g_ref[0] = _sigmoid(_dot(shifted(5), g1_ref[...]))
    lw = jnp.tanh(_dot(shifted(1), w1_ref[...]))
    la = _dot(shifted(4), a1_ref[...])
    lora = w2_ref.shape[1]
    decay_scale = math.exp(-0.5)
    for dirn, (d_ref, a_ref) in enumerate(((df_ref, af_ref), (db_ref, ab_ref))):
        w_raw = w0_ref[dirn:dirn + 1, :] + _dot(lw[:, dirn * lora:(dirn + 1) * lora], w2_ref[dirn])
        _put_tiles(d_ref, jnp.exp(-decay_scale * _sigmoid(w_raw)))
        a_raw = a0_ref[dirn:dirn + 1, :] + _dot(la[:, dirn * lora:(dirn + 1) * lora], a2_ref[dirn])
        _put_tiles(a_ref, _sigmoid(a_raw))


def _rw_proj(x, mod, nw, mu, wr, wk, wv, w1, a1, g1, w2, a2, w0, a0):
    b, t, d = x.shape
    tt = TOKEN_TILE
    nt = t // tt
    hb = tt // SUBLANES
    nhb = t // SUBLANES
    tok = pl.BlockSpec((1, tt, d), lambda bi, i: (bi, i, 0))
    prev = pl.BlockSpec((1, SUBLANES, d), lambda bi, i: (bi, jnp.maximum(i * hb - 1, 0), 0))
    nxt = pl.BlockSpec((1, SUBLANES, d), lambda bi, i: (bi, jnp.minimum((i + 1) * hb, nhb - 1), 0))
    consts = (nw, mu, wr, wk, wv, w1, a1, g1, w2, a2, w0, a0)
    gl = g1.shape[1]
    tiles = _tile_major_shape(b, t, d)
    tile_spec = pl.BlockSpec((1, tt // SUBLANES) + tiles[2:], lambda bi, i: (bi, i, 0, 0))
    out_shape = [jax.ShapeDtypeStruct(tiles, F32)] * 7 + [jax.ShapeDtypeStruct((b, t, gl), F32)]
    out_specs = [tile_spec] * 7 + [pl.BlockSpec((1, tt, gl), lambda bi, i: (bi, i, 0))]
    return pl.pallas_call(
        _rw_proj_kernel,
        grid=(b, nt),
        in_specs=[tok, prev, nxt, _mod_spec(mod)] + [_const_spec(c) for c in consts],
        out_specs=out_specs,
        out_shape=out_shape,
        compiler_params=_params("arbitrary", "arbitrary"),
        name="rw_proj",
    )(x, x, x, mod, *consts)


def _scan_kernel(rf_ref, rm_ref, kf_ref, km_ref, vf_ref, vm_ref, wf_ref, wm_ref, af_ref, am_ref,
                 s0_ref, kk_ref, ka_ref, rk_ref, y_ref, bv_ref, sfin_ref,
                 s_ref, ops_ref, sa_ref, stage_ref, cols_ref):
    c = pl.program_id(1)
    nb, ntb, pairs = rf_ref.shape[0], rf_ref.shape[1], rf_ref.shape[2] // SUBLANES
    tc = ntb * SUBLANES
    hd = s_ref.shape[0]

    @pl.when(c == 0)
    def _():
        s_ref[...] = s0_ref[0]

    operands = ((rf_ref, rm_ref), (kf_ref, km_ref), (vf_ref, vm_ref), (af_ref, am_ref),
                (wf_ref, wm_ref))

    def gather_rows(tb):
        half = tb % 2
        for op, (fwd_ref, bwd_ref) in enumerate(operands):
            for u in range(SUBLANES):
                for b in range(nb):
                    stage_ref[half, op, u, b * pairs:(b + 1) * pairs, :] = (
                        fwd_ref[b, tb, _token_rows(u, pairs), :])
                    stage_ref[half, op, u, (nb + b) * pairs:(nb + b + 1) * pairs, :] = (
                        bwd_ref[b, ntb - 1 - tb, _token_rows(SUBLANES - 1 - u, pairs), :])

    def transpose_rows(s):
        half = (s // SUBLANES) % 2
        u = s % SUBLANES
        for op in range(len(operands)):
            cols = stage_ref[half, op, u].T
            cols_ref[op] = jnp.concatenate([cols[0:hd], cols[hd:2 * hd]], axis=1)

    def prepare(s, slot):
        r = cols_ref[0]
        kraw = cols_ref[1]
        v = cols_ref[2]
        a = cols_ref[3]
        kkv = kraw * kk_ref[...]
        nrm = jnp.sqrt(jnp.sum(kkv * kkv, axis=0, keepdims=True))
        kk = kkv / jnp.maximum(nrm, 1e-12)
        kd = kraw * (1.0 + (a - 1.0) * ka_ref[...])
        p_prev = ops_ref[(s + SCAN_SLOTS - 1) % SCAN_SLOTS, 5]
        p = p_prev * cols_ref[4]
        inv_p = 1.0 / p
        ops_ref[slot, 0] = -kk * p_prev
        ops_ref[slot, 1] = kk * a * inv_p
        ops_ref[slot, 2] = kd * inv_p
        ops_ref[slot, 3] = r * p
        ops_ref[slot, 4] = v
        ops_ref[slot, 5] = p
        bv_ref[0, s] = v * jnp.sum(r * kd * rk_ref[...], axis=0, keepdims=True)

    def accumulate(acc, j, term):
        n = j % SCAN_ACCS
        acc[n] = term if acc[n] is None else acc[n] + term

    def first_sa(slot):
        acc = [None] * SCAN_ACCS
        for j in range(hd):
            accumulate(acc, j, s_ref[j] * ops_ref[slot, 0, j:j + 1, :])
        sa_ref[...] = _tree_sum(acc)

    def advance(s, slot, nxt, midway):
        sa = sa_ref[...]
        vr = ops_ref[slot, 4]
        acc_y = [None] * SCAN_ACCS
        acc_a = [None] * SCAN_ACCS
        for j in range(hd):
            if j == SCAN_PREP_AT:
                midway()
            sn = s_ref[j] + sa * ops_ref[slot, 1, j:j + 1, :] + vr * ops_ref[slot, 2, j:j + 1, :]
            s_ref[j] = sn
            accumulate(acc_y, j, sn * ops_ref[slot, 3, j:j + 1, :])
            accumulate(acc_a, j, sn * ops_ref[nxt, 0, j:j + 1, :])
        y_ref[0, s] = _tree_sum(acc_y)
        sa_ref[...] = _tree_sum(acc_a)

    ops_ref[SCAN_SLOTS - 1, 5] = jnp.ones((hd, LANES), F32)
    gather_rows(0)
    for s in range(2):
        transpose_rows(s)
        prepare(s, s)
    first_sa(0)

    def step(s, carry):
        ahead = jnp.minimum(s + 2, tc - 1)
        transpose_rows(ahead)
        advance(s, s % SCAN_SLOTS, (s + 1) % SCAN_SLOTS,
                functools.partial(prepare, ahead, (s + 2) % SCAN_SLOTS))
        return carry

    def token_block(tb, carry):
        gather_rows(jnp.minimum(tb + 1, ntb - 1))
        return lax.fori_loop(tb * SUBLANES, (tb + 1) * SUBLANES, step, carry)

    lax.fori_loop(0, ntb, token_block, 0)

    for j in range(hd):
        s_ref[j] = s_ref[j] * ops_ref[(tc - 1) % SCAN_SLOTS, 5, j:j + 1, :]

    @pl.when(c == pl.num_programs(1) - 1)
    def _():
        sfin_ref[0] = s_ref[...]


def _scan(r, k, v, wf, wb, af, ab, s0, kk, ka, rk):
    b, tblocks, _, _ = r.shape
    t = tblocks * SUBLANES
    hd = HEAD_DIM
    g = b // SCAN_BATCH
    tc = SCAN_TCHUNK
    nt = t // tc
    blk = (SCAN_BATCH, tc // SUBLANES) + r.shape[2:]
    fwd = pl.BlockSpec(blk, lambda gi, c: (gi, c, 0, 0))
    bwd = pl.BlockSpec(blk, lambda gi, c: (gi, nt - 1 - c, 0, 0))
    step = pl.BlockSpec((1, tc, hd, LANES), lambda gi, c: (gi, c, 0, 0))
    state = pl.BlockSpec((1, hd, hd, LANES), lambda gi, c: (gi, 0, 0, 0))
    steps = jax.ShapeDtypeStruct((g, t, hd, LANES), F32)
    return pl.pallas_call(
        _scan_kernel,
        grid=(g, nt),
        in_specs=[fwd, bwd] * 5 + [state, _const_spec(kk), _const_spec(ka), _const_spec(rk)],
        out_specs=[step, step, state],
        out_shape=[steps, steps, jax.ShapeDtypeStruct((g, hd, hd, LANES), F32)],
        scratch_shapes=[pltpu.VMEM((hd, hd, LANES), F32),
                        pltpu.VMEM((SCAN_SLOTS, 6, hd, LANES), F32),
                        pltpu.VMEM((hd, LANES), F32),
                        pltpu.VMEM((2, 5, SUBLANES, 2 * SCAN_BATCH * r.shape[2] // SUBLANES, LANES), F32),
                        pltpu.VMEM((5, hd, LANES), F32)],
        compiler_params=_params("arbitrary", "arbitrary"),
        name="wkv_scan",
    )(r, r, k, k, v, v, wf, wb, af, ab, s0, kk, ka, rk)


def _rw_post_kernel(y_ref, ym_ref, bv_ref, bvm_ref, lw_ref, lb_ref, lo_ref, hi_ref):
    tp = y_ref.shape[1]
    hd = y_ref.shape[2]
    nb, ntb, pairs = lo_ref.shape[0], lo_ref.shape[1], lo_ref.shape[2] // SUBLANES
    quarter = LANES // 4
    lane = lax.broadcasted_iota(jnp.int32, (hd, LANES), 1)
    fwd_lane = (lane % (2 * quarter)) < quarter

    def swap_dirs(x):
        return jnp.where(fwd_lane, pltpu.roll(x, LANES - quarter, axis=1), pltpu.roll(x, quarter, axis=1))

    def token_block(tb, carry):
        ts = [tb * SUBLANES + u for u in range(SUBLANES)]
        ys = [y_ref[0, t] + swap_dirs(ym_ref[0, tp - 1 - t]) for t in ts]
        bonuses = [bv_ref[0, t] + swap_dirs(bvm_ref[0, tp - 1 - t]) for t in ts]
        outs = []
        for y, bonus in zip(ys, bonuses):
            mean = jnp.mean(y, axis=0, keepdims=True)
            yc = y - mean
            var = jnp.mean(yc * yc, axis=0, keepdims=True)
            yn = yc * lax.rsqrt(var + GN_EPS) * lw_ref[...] + lb_ref[...]
            outs.append(yn + bonus)
        cols = [o.T for o in outs]
        tiles = [jnp.concatenate([c[0:hd], c[hd:2 * hd]], axis=1) for c in cols]
        for u, rows in enumerate(tiles):
            for b in range(nb):
                lo_ref[b, tb, _token_rows(u, pairs), :] = rows[b * pairs:(b + 1) * pairs]
                hi_ref[b, ntb - 1 - tb, _token_rows(SUBLANES - 1 - u, pairs), :] = (
                    rows[(nb + b) * pairs:(nb + b + 1) * pairs])
        return carry

    lax.fori_loop(0, ntb, token_block, 0)


def _rw_post(y, bv, lw, lb, d):
    g, t, hd, _ = y.shape
    tp = POST_TCHUNK
    nb = t // tp
    cur = pl.BlockSpec((1, tp, hd, LANES), lambda gi, c: (gi, c, 0, 0))
    mir = pl.BlockSpec((1, tp, hd, LANES), lambda gi, c: (gi, nb - 1 - c, 0, 0))
    tiles = _tile_major_shape(g * SCAN_BATCH, t // 2, d)
    blk = (SCAN_BATCH, tp // SUBLANES) + tiles[2:]
    lo = pl.BlockSpec(blk, lambda gi, c: (gi, c, 0, 0))
    hi = pl.BlockSpec(blk, lambda gi, c: (gi, nb // 2 - 1 - c, 0, 0))
    half_tokens = jax.ShapeDtypeStruct(tiles, F32)
    return pl.pallas_call(
        _rw_post_kernel,
        grid=(g, nb // 2),
        in_specs=[cur, mir, cur, mir, _const_spec(lw), _const_spec(lb)],
        out_specs=[lo, hi],
        out_shape=[half_tokens, half_tokens],
        compiler_params=_params("arbitrary", "arbitrary"),
        name="rw_post",
    )(y, y, bv, bv, lw, lb)


def _rw_out_kernel(x_ref, yn_ref, g_ref, mod_ref, g2_ref, wo_ref, o_ref):
    d = x_ref.shape[2]
    gate = mod_ref[0, :, 2 * d:3 * d]
    g = _dot(g_ref[0], g2_ref[...])
    out = _dot(_get_tiles(yn_ref) * g, wo_ref[...])
    o_ref[0] = x_ref[0] + gate * out


def _rw_out(x, yn, g, mod, g2, wo):
    b, t, d = x.shape
    tt = min(RW_OUT_TILE, t)
    tok = pl.BlockSpec((1, tt, d), lambda bi, i: (bi, i, 0))
    tiles = pl.BlockSpec((1, tt // SUBLANES) + yn.shape[2:], lambda bi, i: (bi, i, 0, 0))
    gspec = pl.BlockSpec((1, tt, g.shape[2]), lambda bi, i: (bi, i, 0))
    return pl.pallas_call(
        _rw_out_kernel,
        grid=(b, t // tt),
        in_specs=[tok, tiles, gspec, _mod_spec(mod), _const_spec(g2), _const_spec(wo)],
        out_specs=tok,
        out_shape=jax.ShapeDtypeStruct((b, t, d), F32),
        compiler_params=_params("arbitrary", "arbitrary"),
        name="rw_out",
    )(x, yn, g, mod, g2, wo)


def _conv_taps(cur, up, down, w_ref, width, vertical):
    n = cur.shape[0]
    col = lax.broadcasted_iota(jnp.int32, (n, 1), 0) % width
    if vertical:
        ext = jnp.concatenate([up, cur, down], axis=0)
        slabs = [(0, ext[0:n]), (1, cur), (2, ext[2 * width:2 * width + n])]
    else:
        slabs = [(1, cur)]
    z = []
    for dc in range(3):
        acc = None
        for dr, rows in slabs:
            term = rows * w_ref[3 * dr + dc:3 * dr + dc + 1, :]
            acc = term if acc is None else acc + term
        z.append(acc)
    left = jnp.where(col != 0, pltpu.roll(z[0], 1, axis=0), 0.0)
    right = jnp.where(col != width - 1, pltpu.roll(z[2], n - 1, axis=0), 0.0)
    return left + z[1] + right


def _ffn_kernel(*refs, width, vertical, final_norm):
    if vertical:
        (x_ref, xup_ref, xdn_ref, mod_ref, nw_ref, wuv_ref, wug_ref,
         wcv_ref, wcg_ref, bv_ref, bg_ref, wd_ref, nf_ref, o_ref, h_ref, acc_ref) = refs
    else:
        (x_ref, mod_ref, nw_ref, wuv_ref, wug_ref,
         wcv_ref, wcg_ref, bv_ref, bg_ref, wd_ref, nf_ref, o_ref, h_ref, acc_ref) = refs
    i = pl.program_id(1)
    c = pl.program_id(2)
    tt, d = x_ref.shape[1], x_ref.shape[2]
    pad = width if vertical else 0

    @pl.when(c == 0)
    def _():
        shift = mod_ref[0, :, 3 * d:4 * d]
        scale = mod_ref[0, :, 4 * d:5 * d]
        h_ref[pad:pad + tt, :] = _norm_mod(x_ref[0], nw_ref[...], shift, scale).astype(BF16)
        if vertical:
            h_ref[0:pad, :] = _norm_mod(xup_ref[0], nw_ref[...], shift, scale).astype(BF16)
            h_ref[pad + tt:, :] = _norm_mod(xdn_ref[0], nw_ref[...], shift, scale).astype(BF16)

    top = i == 0
    bottom = i == pl.num_programs(1) - 1

    def conv(w_up_ref, w_conv_ref):
        up = jnp.dot(h_ref[...], w_up_ref[...], preferred_element_type=F32)
        if not vertical:
            return _conv_taps(up, None, None, w_conv_ref, width, vertical)
        above = jnp.where(top, 0.0, up[0:pad])
        below = jnp.where(bottom, 0.0, up[pad + tt:])
        return _conv_taps(up[pad:pad + tt], above, below, w_conv_ref, width, vertical)

    val = conv(wuv_ref, wcv_ref) + bv_ref[...]
    gat = conv(wug_ref, wcg_ref) + bg_ref[...]
    act = gat * _sigmoid(gat) * val
    part = _dot(act, wd_ref[...])

    @pl.when(c == 0)
    def _():
        acc_ref[...] = part

    @pl.when(c != 0)
    def _():
        acc_ref[...] += part

    @pl.when(c == pl.num_programs(2) - 1)
    def _():
        gate = mod_ref[0, :, 5 * d:6 * d]
        y = x_ref[0] + gate * acc_ref[...]
        if final_norm:
            ms = jnp.mean(y * y, axis=-1, keepdims=True)
            y = y * lax.rsqrt(ms + RMS_EPS) * nf_ref[...]
        o_ref[0] = y


def _ffn(x, mod, nw, w_up, w_conv, b_conv, w_down, nf, *, width, final_norm):
    out_shape = x.shape
    vertical = width < x.shape[1]
    merge = FFN_TILE // x.shape[1]
    if not vertical and mod.shape[0] == 1 and merge > 1 and x.shape[0] % merge == 0:
        x = x.reshape(x.shape[0] // merge, merge * x.shape[1], x.shape[2])
    b, t, d = x.shape
    f = w_down.shape[0]
    tt = min(FFN_TILE, t)
    nt = t // tt
    ck = f // FFN_CHANNEL_BLOCKS
    nc = FFN_CHANNEL_BLOCKS
    wc = w_conv.reshape(9, 2 * f)
    bc = b_conv.reshape(1, 2 * f)
    hb = tt // width if vertical else 1
    nhb = t // width if vertical else 1

    if mod.shape[0] == 1:
        mspec = pl.BlockSpec((1, 1, mod.shape[2]), lambda bi, i, c: (0, 0, 0))
    else:
        mspec = pl.BlockSpec((1, 1, mod.shape[2]), lambda bi, i, c: (bi, 0, 0))
    tok = pl.BlockSpec((1, tt, d), lambda bi, i, c: (bi, i, 0))
    xs = [x]
    x_specs = [tok]
    if vertical:
        xs += [x, x]
        x_specs += [
            pl.BlockSpec((1, width, d), lambda bi, i, c: (bi, jnp.maximum(i * hb - 1, 0), 0)),
            pl.BlockSpec((1, width, d), lambda bi, i, c: (bi, jnp.minimum((i + 1) * hb, nhb - 1), 0)),
        ]
    in_specs = x_specs + [
        mspec,
        pl.BlockSpec(nw.shape, lambda bi, i, c: (0, 0)),
        pl.BlockSpec((d, ck), lambda bi, i, c: (0, c)),
        pl.BlockSpec((d, ck), lambda bi, i, c: (0, c + nc)),
        pl.BlockSpec((9, ck), lambda bi, i, c: (0, c)),
        pl.BlockSpec((9, ck), lambda bi, i, c: (0, c + nc)),
        pl.BlockSpec((1, ck), lambda bi, i, c: (0, c)),
        pl.BlockSpec((1, ck), lambda bi, i, c: (0, c + nc)),
        pl.BlockSpec((ck, d), lambda bi, i, c: (c, 0)),
        pl.BlockSpec(nf.shape, lambda bi, i, c: (0, 0)),
    ]
    rows = tt + 2 * width if vertical else tt
    kern = functools.partial(_ffn_kernel, width=width, vertical=vertical, final_norm=final_norm)
    out = pl.pallas_call(
        kern,
        grid=(b, nt, nc),
        in_specs=in_specs,
        out_specs=tok,
        out_shape=jax.ShapeDtypeStruct((b, t, d), F32),
        scratch_shapes=[pltpu.VMEM((rows, d), BF16), pltpu.VMEM((tt, d), F32)],
        compiler_params=_params("arbitrary", "arbitrary", "arbitrary"),
        name="conv_ffn",
    )(*xs, mod, nw, w_up, w_up, wc, wc, bc, bc, w_down, nf)
    return out.reshape(out_shape)


def _sgu_kernel(x_ref, mod_ref, nw_ref, win_ref, lnw_ref, lnb_ref, ws_ref, bs_ref, wout_ref, o_ref):
    d = x_ref.shape[2]
    tt = x_ref.shape[1]
    e = wout_ref.shape[0]
    groups = ws_ref.shape[0]
    gw = e // groups
    shift = mod_ref[0, :, 0:d]
    scale = mod_ref[0, :, d:2 * d]
    gate = mod_ref[0, :, 2 * d:3 * d]
    chunk_rows = [slice(p * SGU_CHUNK, (p + 1) * SGU_CHUNK) for p in range(tt // SGU_CHUNK)]
    xs = [x_ref[0, rows, :] for rows in chunk_rows]
    zs = [_dot(_norm_mod(x, nw_ref[...], shift, scale), win_ref[...]) for x in xs]
    for rows, x, z in zip(chunk_rows, xs, zs):
        z = 0.5 * z * (1.0 + lax.erf(z * (1.0 / math.sqrt(2.0))))
        u = z[:, :e]
        v = z[:, e:]
        mu = jnp.mean(v, axis=-1, keepdims=True)
        vc = v - mu
        var = jnp.mean(vc * vc, axis=-1, keepdims=True)
        vn = (vc * lax.rsqrt(var + LN_EPS) * lnw_ref[...] + lnb_ref[...]).astype(BF16)
        parts = [jnp.dot(ws_ref[gi], vn[:, gi * gw:(gi + 1) * gw], preferred_element_type=F32)
                 for gi in range(groups)]
        vm = jnp.concatenate(parts, axis=1) + bs_ref[...]
        out = _dot(u * vm, wout_ref[...])
        o_ref[0, rows, :] = x + gate * out


def _sgu(x, mod, nw, w_in, ln_w, ln_b, w_s, b_full, w_out):
    b, t, d = x.shape
    tt = TOKEN_TILE
    tok = pl.BlockSpec((1, tt, d), lambda bi, i: (bi, i, 0))
    consts = (nw, w_in, ln_w, ln_b, w_s, b_full, w_out)
    return pl.pallas_call(
        _sgu_kernel,
        grid=(b, t // tt),
        in_specs=[tok, _mod_spec(mod)] + [_const_spec(c) for c in consts],
        out_specs=tok,
        out_shape=jax.ShapeDtypeStruct((b, t, d), F32),
        compiler_params=_params("arbitrary", "arbitrary"),
        name="sgu",
    )(x, mod, *consts)


def _lane_param(p, heads):
    m = p.reshape(heads // 2, 2, HEAD_DIM).transpose(2, 1, 0)
    m = jnp.broadcast_to(m[:, :, None, None, :], (HEAD_DIM, 2, 2, SCAN_BATCH, heads // 2))
    return m.reshape(HEAD_DIM, LANES)


def _state_to_scan(sf, sb):
    b, h = sf.shape[:2]
    g = b // SCAN_BATCH
    s = jnp.stack([sf, sb], axis=0).reshape(2, g, SCAN_BATCH, h // 2, 2, HEAD_DIM, HEAD_DIM)
    return s.transpose(1, 6, 5, 4, 0, 2, 3).reshape(g, HEAD_DIM, HEAD_DIM, LANES)


def _state_from_scan(s, heads):
    g = s.shape[0]
    s = s.reshape(g, HEAD_DIM, HEAD_DIM, 2, 2, SCAN_BATCH, heads // 2)
    s = s.transpose(4, 0, 5, 6, 3, 2, 1).reshape(2, g * SCAN_BATCH, heads, HEAD_DIM, HEAD_DIM)
    return s[0], s[1]


def kernel(x_prompt, x_sample, state_ctx_fwd, state_ctx_bwd, c, c_ctx, ada_w, ada_b, norm_mix, norm_ffn, ffn_up, ffn_conv, ffn_conv_b, ffn_down, norm_final, rw_mu, rw_wr, rw_wk, rw_wv, rw_wo, rw_w0, rw_w1, rw_w2, rw_a0, rw_a1, rw_a2, rw_g1, rw_g2, rw_kk, rw_ka, rw_rk, rw_lnx_w, rw_lnx_b, sg_in, sg_ln_w, sg_ln_b, sg_ws, sg_bs, sg_out):
    depth, d, _ = ada_w.shape
    heads = d // HEAD_DIM
    n_dec = c.shape[0]
    assert SCAN_BATCH * heads * 2 == LANES
    assert n_dec + 1 <= SUBLANES

    cond8 = jnp.concatenate(
        [c_ctx[None, :], c, jnp.zeros((SUBLANES - 1 - n_dec, d), F32)], axis=0)
    mod_all = _modulation(cond8, ada_w, ada_b)
    nf = norm_final.reshape(1, d)

    def run_stream(x, mod_rows, s0f, s0b, width, want_state):
        b, t, _ = x.shape
        new_f, new_b = [], []
        for i in range(depth):
            mod = mod_all[i, mod_rows[0]:mod_rows[1]][:, None, :]
            j = i // 2
            nw_mix = norm_mix[i].reshape(1, d)
            if i % 2 == 0:
                w1 = jnp.concatenate([rw_w1[j, 0], rw_w1[j, 1]], axis=1).astype(BF16)
                a1 = jnp.concatenate([rw_a1[j, 0], rw_a1[j, 1]], axis=1).astype(BF16)
                r, k, v, df, db, af, ab, g = _rw_proj(
                    x, mod, nw_mix, rw_mu[j],
                    rw_wr[j].astype(BF16), rw_wk[j].astype(BF16), rw_wv[j].astype(BF16),
                    w1, a1, rw_g1[j].astype(BF16),
                    rw_w2[j].astype(BF16), rw_a2[j].astype(BF16), rw_w0[j], rw_a0[j])
                s0 = _state_to_scan(s0f[:, j], s0b[:, j])
                ys, bvs, sfin = _scan(
                    r, k, v, df, db, af, ab, s0,
                    _lane_param(rw_kk[j], heads), _lane_param(rw_ka[j], heads),
                    _lane_param(rw_rk[j].reshape(-1), heads))
                lo, hi = _rw_post(ys, bvs, _lane_param(rw_lnx_w[j], heads),
                                  _lane_param(rw_lnx_b[j], heads), d)
                yn = jnp.concatenate([lo, hi], axis=1)
                x = _rw_out(x, yn, g, mod, rw_g2[j].astype(BF16), rw_wo[j].astype(BF16))
                if want_state:
                    sf, sb = _state_from_scan(sfin, heads)
                    new_f.append(sf)
                    new_b.append(sb)
            else:
                e = sg_out.shape[1]
                b_full = jnp.repeat(sg_bs[j].T, e // sg_bs.shape[1], axis=1)
                x = _sgu(x, mod, nw_mix, sg_in[j].astype(BF16),
                         sg_ln_w[j].reshape(1, e), sg_ln_b[j].reshape(1, e),
                         sg_ws[j].astype(BF16), b_full, sg_out[j].astype(BF16))
            x = _ffn(x, mod, norm_ffn[i].reshape(1, d), ffn_up[i].astype(BF16), ffn_conv[i],
                     ffn_conv_b[i], ffn_down[i].astype(BF16), nf,
                     width=width, final_norm=(i == depth - 1))
        if want_state:
            return x, jnp.stack(new_f, axis=1), jnp.stack(new_b, axis=1)
        return x, None, None

    bp = x_prompt.shape[0]
    n_rwkv = state_ctx_fwd.shape[1]
    zero_state = jnp.zeros((bp, n_rwkv, heads, HEAD_DIM, HEAD_DIM), F32)
    y_prompt, new_f, new_b = run_stream(x_prompt, (0, 1), zero_state, zero_state,
                                        x_prompt.shape[1], True)
    y_sample, _, _ = run_stream(x_sample, (1, 1 + n_dec), state_ctx_fwd, state_ctx_bwd,
                                GRID_W, False)
    return (y_prompt, y_sample, new_f, new_b)
```

```python
import functools
import math

import jax
import jax.numpy as jnp
from jax import lax
from jax.experimental import pallas as pl
from jax.experimental.pallas import tpu as pltpu

RMS_EPS = 1e-6
LN_EPS = 1e-5
GN_EPS = 64e-5
HEAD_DIM = 64
SGU_CHUNK = 128
GRID_W = 64

LANES = 128
SUBLANES = 8
VMEM_LIMIT = 56 * 1024 * 1024

TOKEN_TILE = 256
RW_OUT_TILE = 512
FFN_TILE = 512
FFN_CHANNEL_BLOCKS = 2
SCAN_BATCH = 4
SCAN_TCHUNK = 64
SCAN_PREP_AT = 52
SCAN_ACCS = 1
SCAN_SLOTS = 3
POST_TCHUNK = 64

BF16 = jnp.bfloat16
F32 = jnp.float32


def _params(*sem):
    return pltpu.CompilerParams(dimension_semantics=sem, vmem_limit_bytes=VMEM_LIMIT)


def _dot(a, b):
    return jnp.dot(a.astype(BF16), b, preferred_element_type=F32)


def _norm_mod(x, g, shift, scale):
    ms = jnp.mean(x * x, axis=-1, keepdims=True)
    return x * lax.rsqrt(ms + RMS_EPS) * g * (1.0 + scale) + shift


def _sigmoid(x):
    return 1.0 / (1.0 + jnp.exp(-x))


def _tree_sum(xs):
    while len(xs) > 1:
        xs = [xs[i] + xs[i + 1] for i in range(0, len(xs), 2)]
    return xs[0]


def _tile_major_shape(b, t, d):
    return (b, t // SUBLANES, d // LANES * SUBLANES, LANES)


def _put_tiles(ref, val):
    n = val.shape[0] // SUBLANES
    for p in range(val.shape[1] // LANES):
        ref[0, :, p * SUBLANES:(p + 1) * SUBLANES, :] = (
            val[:, p * LANES:(p + 1) * LANES].reshape(n, SUBLANES, LANES))


def _get_tiles(ref):
    _, n, rows, _ = ref.shape
    return jnp.concatenate(
        [ref[0, :, p * SUBLANES:(p + 1) * SUBLANES, :].reshape(n * SUBLANES, LANES)
         for p in range(rows // SUBLANES)], axis=1)


def _token_rows(u, pairs):
    return pl.ds(u, pairs, stride=SUBLANES)


def _mod_kernel(c_ref, w_ref, b_ref, o_ref):
    c = c_ref[...]
    sc = c * _sigmoid(c)
    o_ref[0] = jnp.dot(sc, w_ref[0], preferred_element_type=F32,
                       precision=lax.Precision.HIGHEST) + b_ref[0]


def _modulation(cond8, ada_w, ada_b):
    depth, d, n = ada_w.shape
    tn = n // 4
    return pl.pallas_call(
        _mod_kernel,
        grid=(depth, n // tn),
        in_specs=[
            pl.BlockSpec((SUBLANES, d), lambda l, j: (0, 0)),
            pl.BlockSpec((1, d, tn), lambda l, j: (l, 0, j)),
            pl.BlockSpec((1, 1, tn), lambda l, j: (l, 0, j)),
        ],
        out_specs=pl.BlockSpec((1, SUBLANES, tn), lambda l, j: (l, 0, j)),
        out_shape=jax.ShapeDtypeStruct((depth, SUBLANES, n), F32),
        compiler_params=_params("arbitrary", "arbitrary"),
        name="modulation",
    )(cond8, ada_w, ada_b.reshape(depth, 1, n))


def _mod_spec(mod):
    if mod.shape[0] == 1:
        return pl.BlockSpec((1, 1, mod.shape[2]), lambda b, i: (0, 0, 0))
    return pl.BlockSpec((1, 1, mod.shape[2]), lambda b, i: (b, 0, 0))


def _const_spec(a):
    nd = a.ndim
    return pl.BlockSpec(a.shape, lambda *_: (0,) * nd)


def _rw_proj_kernel(x_ref, xp_ref, xn_ref, mod_ref, nw_ref, mu_ref,
                    wr_ref, wk_ref, wv_ref, w1_ref, a1_ref, g1_ref,
                    w2_ref, a2_ref, w0_ref, a0_ref,
                    r_ref, k_ref, v_ref, df_ref, db_ref, af_ref, ab_ref, g_ref):
    i = pl.program_id(1)
    last = pl.num_programs(1) - 1
    d = x_ref.shape[2]
    tt = x_ref.shape[1]
    nw = nw_ref[...]
    shift = mod_ref[0, :, 0:d]
    scale = mod_ref[0, :, d:2 * d]
    h = _norm_mod(x_ref[0], nw, shift, scale)
    hp = _norm_mod(xp_ref[0], nw, shift, scale)[SUBLANES - 1:SUBLANES, :]
    hn = _norm_mod(xn_ref[0], nw, shift, scale)[0:1, :]
    hp = jnp.where(i == 0, 0.0, hp)
    hn = jnp.where(i == last, 0.0, hn)
    rows = lax.broadcasted_iota(jnp.int32, (tt, 1), 0)
    prev = jnp.where(rows == 0, hp, pltpu.roll(h, 1, axis=0))
    nxt = jnp.where(rows == tt - 1, hn, pltpu.roll(h, tt - 1, axis=0))
    dp = prev - h
    dn = nxt - h

    def shifted(idx):
        return h + dp * mu_ref[0, idx:idx + 1, :] + dn * mu_ref[1, idx:idx + 1, :]

    _put_tiles(r_ref, _dot(shifted(0), wr_ref[...]))
    _put_tiles(k_ref, _dot(shifted(2), wk_ref[...]))
    _put_tiles(v_ref, _dot(shifted(3), wv_ref[...]))
    g_ref[0] = _sigmoid(_dot(shifted(5), g1_ref[...]))
    lw = jnp.tanh(_dot(shifted(1), w1_ref[...]))
    la = _dot(shifted(4), a1_ref[...])
    lora = w2_ref.shape[1]
    decay_scale = math.exp(-0.5)
    for dirn, (d_ref, a_ref) in enumerate(((df_ref, af_ref), (db_ref, ab_ref))):
        w_raw = w0_ref[dirn:dirn + 1, :] + _dot(lw[:, dirn * lora:(dirn + 1) * lora], w2_ref[dirn])
        _put_tiles(d_ref, jnp.exp(-decay_scale * _sigmoid(w_raw)))
        a_raw = a0_ref[dirn:dirn + 1, :] + _dot(la[:, dirn * lora:(dirn + 1) * lora], a2_ref[dirn])
        _put_tiles(a_ref, _sigmoid(a_raw))


def _rw_proj(x, mod, nw, mu, wr, wk, wv, w1, a1, g1, w2, a2, w0, a0):
    b, t, d = x.shape
    tt = TOKEN_TILE
    nt = t // tt
    hb = tt // SUBLANES
    nhb = t // SUBLANES
    tok = pl.BlockSpec((1, tt, d), lambda bi, i: (bi, i, 0))
    prev = pl.BlockSpec((1, SUBLANES, d), lambda bi, i: (bi, jnp.maximum(i * hb - 1, 0), 0))
    nxt = pl.BlockSpec((1, SUBLANES, d), lambda bi, i: (bi, jnp.minimum((i + 1) * hb, nhb - 1), 0))
    consts = (nw, mu, wr, wk, wv, w1, a1, g1, w2, a2, w0, a0)
    gl = g1.shape[1]
    tiles = _tile_major_shape(b, t, d)
    tile_spec = pl.BlockSpec((1, tt // SUBLANES) + tiles[2:], lambda bi, i: (bi, i, 0, 0))
    out_shape = [jax.ShapeDtypeStruct(tiles, F32)] * 7 + [jax.ShapeDtypeStruct((b, t, gl), F32)]
    out_specs = [tile_spec] * 7 + [pl.BlockSpec((1, tt, gl), lambda bi, i: (bi, i, 0))]
    return pl.pallas_call(
        _rw_proj_kernel,
        grid=(b, nt),
        in_specs=[tok, prev, nxt, _mod_spec(mod)] + [_const_spec(c) for c in consts],
        out_specs=out_specs,
        out_shape=out_shape,
        compiler_params=_params("arbitrary", "arbitrary"),
        name="rw_proj",
    )(x, x, x, mod, *consts)


def _scan_kernel(rf_ref, rm_ref, kf_ref, km_ref, vf_ref, vm_ref, wf_ref, wm_ref, af_ref, am_ref,
                 s0_ref, kk_ref, ka_ref, rk_ref, y_ref, bv_ref, sfin_ref,
                 s_ref, ops_ref, sa_ref, stage_ref, cols_ref):
    c = pl.program_id(1)
    nb, ntb, pairs = rf_ref.shape[0], rf_ref.shape[1], rf_ref.shape[2] // SUBLANES
    tc = ntb * SUBLANES
    hd = s_ref.shape[0]

    @pl.when(c == 0)
    def _():
        s_ref[...] = s0_ref[0]

    operands = ((rf_ref, rm_ref), (kf_ref, km_ref), (vf_ref, vm_ref), (af_ref, am_ref),
                (wf_ref, wm_ref))

    def gather_rows(tb):
        half = tb % 2
        for op, (fwd_ref, bwd_ref) in enumerate(operands):
            for b in range(nb):
                tiles = fwd_ref[b, tb].reshape(pairs, SUBLANES, LANES)
                stage_ref[half, op, :, b * pairs:(b + 1) * pairs, :] = jnp.swapaxes(tiles, 0, 1)
                tiles = bwd_ref[b, ntb - 1 - tb].reshape(pairs, SUBLANES, LANES)
                by_token = jnp.swapaxes(tiles, 0, 1)
                for u in range(SUBLANES):
                    stage_ref[half, op, SUBLANES - 1 - u, (nb + b) * pairs:(nb + b + 1) * pairs, :] = (
                        by_token[u])

    def transpose_rows(s):
        half = (s // SUBLANES) % 2
        u = s % SUBLANES
        for op in range(len(operands)):
            cols = stage_ref[half, op, u].T
            cols_ref[op] = jnp.concatenate([cols[0:hd], cols[hd:2 * hd]], axis=1)

    def prepare(s, slot):
        r = cols_ref[0]
        kraw = cols_ref[1]
        v = cols_ref[2]
        a = cols_ref[3]
        kkv = kraw * kk_ref[...]
        nrm = jnp.sqrt(jnp.sum(kkv * kkv, axis=0, keepdims=True))
        kk = kkv / jnp.maximum(nrm, 1e-12)
        kd = kraw * (1.0 + (a - 1.0) * ka_ref[...])
        p_prev = ops_ref[(s + SCAN_SLOTS - 1) % SCAN_SLOTS, 5]
        p = p_prev * cols_ref[4]
        inv_p = 1.0 / p
        ops_ref[slot, 0] = -kk * p_prev
        ops_ref[slot, 1] = kk * a * inv_p
        ops_ref[slot, 2] = kd * inv_p
        ops_ref[slot, 3] = r * p
        ops_ref[slot, 4] = v
        ops_ref[slot, 5] = p
        bv_ref[0, s] = v * jnp.sum(r * kd * rk_ref[...], axis=0, keepdims=True)

    def accumulate(acc, j, term):
        n = j % SCAN_ACCS
        acc[n] = term if acc[n] is None else acc[n] + term

    def first_sa(slot):
        acc = [None] * SCAN_ACCS
        for j in range(hd):
            accumulate(acc, j, s_ref[j] * ops_ref[slot, 0, j:j + 1, :])
        sa_ref[...] = _tree_sum(acc)

    def advance(s, slot, nxt, midway):
        sa = sa_ref[...]
        vr = ops_ref[slot, 4]
        acc_y = [None] * SCAN_ACCS
        acc_a = [None] * SCAN_ACCS
        for j in range(hd):
            if j == SCAN_PREP_AT:
                midway()
            sn = s_ref[j] + sa * ops_ref[slot, 1, j:j + 1, :] + vr * ops_ref[slot, 2, j:j + 1, :]
            s_ref[j] = sn
            accumulate(acc_y, j, sn * ops_ref[slot, 3, j:j + 1, :])
            accumulate(acc_a, j, sn * ops_ref[nxt, 0, j:j + 1, :])
        y_ref[0, s] = _tree_sum(acc_y)
        sa_ref[...] = _tree_sum(acc_a)

    ops_ref[SCAN_SLOTS - 1, 5] = jnp.ones((hd, LANES), F32)
    gather_rows(0)
    for s in range(2):
        transpose_rows(s)
        prepare(s, s)
    first_sa(0)

    def step(s, carry):
        ahead = jnp.minimum(s + 2, tc - 1)
        transpose_rows(ahead)
        advance(s, s % SCAN_SLOTS, (s + 1) % SCAN_SLOTS,
                functools.partial(prepare, ahead, (s + 2) % SCAN_SLOTS))
        return carry

    def token_block(tb, carry):
        gather_rows(jnp.minimum(tb + 1, ntb - 1))
        return lax.fori_loop(tb * SUBLANES, (tb + 1) * SUBLANES, step, carry)

    lax.fori_loop(0, ntb, token_block, 0)

    for j in range(hd):
        s_ref[j] = s_ref[j] * ops_ref[(tc - 1) % SCAN_SLOTS, 5, j:j + 1, :]

    @pl.when(c == pl.num_programs(1) - 1)
    def _():
        sfin_ref[0] = s_ref[...]


def _scan(r, k, v, wf, wb, af, ab, s0, kk, ka, rk):
    b, tblocks, _, _ = r.shape
    t = tblocks * SUBLANES
    hd = HEAD_DIM
    g = b // SCAN_BATCH
    tc = SCAN_TCHUNK
    nt = t // tc
    blk = (SCAN_BATCH, tc // SUBLANES) + r.shape[2:]
    fwd = pl.BlockSpec(blk, lambda gi, c: (gi, c, 0, 0))
    bwd = pl.BlockSpec(blk, lambda gi, c: (gi, nt - 1 - c, 0, 0))
    step = pl.BlockSpec((1, tc, hd, LANES), lambda gi, c: (gi, c, 0, 0))
    state = pl.BlockSpec((1, hd, hd, LANES), lambda gi, c: (gi, 0, 0, 0))
    steps = jax.ShapeDtypeStruct((g, t, hd, LANES), F32)
    return pl.pallas_call(
        _scan_kernel,
        grid=(g, nt),
        in_specs=[fwd, bwd] * 5 + [state, _const_spec(kk), _const_spec(ka), _const_spec(rk)],
        out_specs=[step, step, state],
        out_shape=[steps, steps, jax.ShapeDtypeStruct((g, hd, hd, LANES), F32)],
        scratch_shapes=[pltpu.VMEM((hd, hd, LANES), F32),
                        pltpu.VMEM((SCAN_SLOTS, 6, hd, LANES), F32),
                        pltpu.VMEM((hd, LANES), F32),
                        pltpu.VMEM((2, 5, SUBLANES, 2 * SCAN_BATCH * r.shape[2] // SUBLANES, LANES), F32),
                        pltpu.VMEM((5, hd, LANES), F32)],
        compiler_params=_params("arbitrary", "arbitrary"),
        name="wkv_scan",
    )(r, r, k, k, v, v, wf, wb, af, ab, s0, kk, ka, rk)


def _rw_post_kernel(y_ref, ym_ref, bv_ref, bvm_ref, lw_ref, lb_ref, lo_ref, hi_ref):
    tp = y_ref.shape[1]
    hd = y_ref.shape[2]
    nb, ntb, pairs = lo_ref.shape[0], lo_ref.shape[1], lo_ref.shape[2] // SUBLANES
    quarter = LANES // 4
    lane = lax.broadcasted_iota(jnp.int32, (hd, LANES), 1)
    fwd_lane = (lane % (2 * quarter)) < quarter

    def swap_dirs(x):
        return jnp.where(fwd_lane, pltpu.roll(x, LANES - quarter, axis=1), pltpu.roll(x, quarter, axis=1))

    def token_block(tb, carry):
        ts = [tb * SUBLANES + u for u in range(SUBLANES)]
        ys = [y_ref[0, t] + swap_dirs(ym_ref[0, tp - 1 - t]) for t in ts]
        bonuses = [bv_ref[0, t] + swap_dirs(bvm_ref[0, tp - 1 - t]) for t in ts]
        outs = []
        for y, bonus in zip(ys, bonuses):
            mean = jnp.mean(y, axis=0, keepdims=True)
            yc = y - mean
            var = jnp.mean(yc * yc, axis=0, keepdims=True)
            yn = yc * lax.rsqrt(var + GN_EPS) * lw_ref[...] + lb_ref[...]
            outs.append(yn + bonus)
        cols = [o.T for o in outs]
        tiles = [jnp.concatenate([c[0:hd], c[hd:2 * hd]], axis=1) for c in cols]
        for u, rows in enumerate(tiles):
            for b in range(nb):
                lo_ref[b, tb, _token_rows(u, pairs), :] = rows[b * pairs:(b + 1) * pairs]
                hi_ref[b, ntb - 1 - tb, _token_rows(SUBLANES - 1 - u, pairs), :] = (
                    rows[(nb + b) * pairs:(nb + b + 1) * pairs])
        return carry

    lax.fori_loop(0, ntb, token_block, 0)


def _rw_post(y, bv, lw, lb, d):
    g, t, hd, _ = y.shape
    tp = POST_TCHUNK
    nb = t // tp
    cur = pl.BlockSpec((1, tp, hd, LANES), lambda gi, c: (gi, c, 0, 0))
    mir = pl.BlockSpec((1, tp, hd, LANES), lambda gi, c: (gi, nb - 1 - c, 0, 0))
    tiles = _tile_major_shape(g * SCAN_BATCH, t // 2, d)
    blk = (SCAN_BATCH, tp // SUBLANES) + tiles[2:]
    lo = pl.BlockSpec(blk, lambda gi, c: (gi, c, 0, 0))
    hi = pl.BlockSpec(blk, lambda gi, c: (gi, nb // 2 - 1 - c, 0, 0))
    half_tokens = jax.ShapeDtypeStruct(tiles, F32)
    return pl.pallas_call(
        _rw_post_kernel,
        grid=(g, nb // 2),
        in_specs=[cur, mir, cur, mir, _const_spec(lw), _const_spec(lb)],
        out_specs=[lo, hi],
        out_shape=[half_tokens, half_tokens],
        compiler_params=_params("arbitrary", "arbitrary"),
        name="rw_post",
    )(y, y, bv, bv, lw, lb)


def _rw_out_kernel(x_ref, yn_ref, g_ref, mod_ref, g2_ref, wo_ref, o_ref):
    d = x_ref.shape[2]
    gate = mod_ref[0, :, 2 * d:3 * d]
    g = _dot(g_ref[0], g2_ref[...])
    out = _dot(_get_tiles(yn_ref) * g, wo_ref[...])
    o_ref[0] = x_ref[0] + gate * out


def _rw_out(x, yn, g, mod, g2, wo):
    b, t, d = x.shape
    tt = min(RW_OUT_TILE, t)
    tok = pl.BlockSpec((1, tt, d), lambda bi, i: (bi, i, 0))
    tiles = pl.BlockSpec((1, tt // SUBLANES) + yn.shape[2:], lambda bi, i: (bi, i, 0, 0))
    gspec = pl.BlockSpec((1, tt, g.shape[2]), lambda bi, i: (bi, i, 0))
    return pl.pallas_call(
        _rw_out_kernel,
        grid=(b, t // tt),
        in_specs=[tok, tiles, gspec, _mod_spec(mod), _const_spec(g2), _const_spec(wo)],
        out_specs=tok,
        out_shape=jax.ShapeDtypeStruct((b, t, d), F32),
        compiler_params=_params("arbitrary", "arbitrary"),
        name="rw_out",
    )(x, yn, g, mod, g2, wo)


def _conv_taps(cur, up, down, w_ref, width, vertical):
    n = cur.shape[0]
    col = lax.broadcasted_iota(jnp.int32, (n, 1), 0) % width
    if vertical:
        ext = jnp.concatenate([up, cur, down], axis=0)
        slabs = [(0, ext[0:n]), (1, cur), (2, ext[2 * width:2 * width + n])]
    else:
        slabs = [(1, cur)]
    z = []
    for dc in range(3):
        acc = None
        for dr, rows in slabs:
            term = rows * w_ref[3 * dr + dc:3 * dr + dc + 1, :]
            acc = term if acc is None else acc + term
        z.append(acc)
    left = jnp.where(col != 0, pltpu.roll(z[0], 1, axis=0), 0.0)
    right = jnp.where(col != width - 1, pltpu.roll(z[2], n - 1, axis=0), 0.0)
    return left + z[1] + right


def _ffn_kernel(*refs, width, vertical, final_norm):
    if vertical:
        (x_ref, xup_ref, xdn_ref, mod_ref, nw_ref, wuv_ref, wug_ref,
         wcv_ref, wcg_ref, bv_ref, bg_ref, wd_ref, nf_ref, o_ref, h_ref, acc_ref) = refs
    else:
        (x_ref, mod_ref, nw_ref, wuv_ref, wug_ref,
         wcv_ref, wcg_ref, bv_ref, bg_ref, wd_ref, nf_ref, o_ref, h_ref, acc_ref) = refs
    i = pl.program_id(1)
    c = pl.program_id(2)
    tt, d = x_ref.shape[1], x_ref.shape[2]
    pad = width if vertical else 0

    @pl.when(c == 0)
    def _():
        shift = mod_ref[0, :, 3 * d:4 * d]
        scale = mod_ref[0, :, 4 * d:5 * d]
        h_ref[pad:pad + tt, :] = _norm_mod(x_ref[0], nw_ref[...], shift, scale).astype(BF16)
        if vertical:
            h_ref[0:pad, :] = _norm_mod(xup_ref[0], nw_ref[...], shift, scale).astype(BF16)
            h_ref[pad + tt:, :] = _norm_mod(xdn_ref[0], nw_ref[...], shift, scale).astype(BF16)

    top = i == 0
    bottom = i == pl.num_programs(1) - 1

    def conv(w_up_ref, w_conv_ref):
        up = jnp.dot(h_ref[...], w_up_ref[...], preferred_element_type=F32)
        if not vertical:
            return _conv_taps(up, None, None, w_conv_ref, width, vertical)
        above = jnp.where(top, 0.0, up[0:pad])
        below = jnp.where(bottom, 0.0, up[pad + tt:])
        return _conv_taps(up[pad:pad + tt], above, below, w_conv_ref, width, vertical)

    val = conv(wuv_ref, wcv_ref) + bv_ref[...]
    gat = conv(wug_ref, wcg_ref) + bg_ref[...]
    act = gat * _sigmoid(gat) * val
    part = _dot(act, wd_ref[...])

    @pl.when(c == 0)
    def _():
        acc_ref[...] = part

    @pl.when(c != 0)
    def _():
        acc_ref[...] += part

    @pl.when(c == pl.num_programs(2) - 1)
    def _():
        gate = mod_ref[0, :, 5 * d:6 * d]
        y = x_ref[0] + gate * acc_ref[...]
        if final_norm:
            ms = jnp.mean(y * y, axis=-1, keepdims=True)
            y = y * lax.rsqrt(ms + RMS_EPS) * nf_ref[...]
        o_ref[0] = y


def _ffn(x, mod, nw, w_up, w_conv, b_conv, w_down, nf, *, width, final_norm):
    out_shape = x.shape
    vertical = width < x.shape[1]
    merge = FFN_TILE // x.shape[1]
    if not vertical and mod.shape[0] == 1 and merge > 1 and x.shape[0] % merge == 0:
        x = x.reshape(x.shape[0] // merge, merge * x.shape[1], x.shape[2])
    b, t, d = x.shape
    f = w_down.shape[0]
    tt = min(FFN_TILE, t)
    nt = t // tt
    ck = f // FFN_CHANNEL_BLOCKS
    nc = FFN_CHANNEL_BLOCKS
    wc = w_conv.reshape(9, 2 * f)
    bc = b_conv.reshape(1, 2 * f)
    hb = tt // width if vertical else 1
    nhb = t // width if vertical else 1

    if mod.shape[0] == 1:
        mspec = pl.BlockSpec((1, 1, mod.shape[2]), lambda bi, i, c: (0, 0, 0))
    else:
        mspec = pl.BlockSpec((1, 1, mod.shape[2]), lambda bi, i, c: (bi, 0, 0))
    tok = pl.BlockSpec((1, tt, d), lambda bi, i, c: (bi, i, 0))
    xs = [x]
    x_specs = [tok]
    if vertical:
        xs += [x, x]
        x_specs += [
            pl.BlockSpec((1, width, d), lambda bi, i, c: (bi, jnp.maximum(i * hb - 1, 0), 0)),
            pl.BlockSpec((1, width, d), lambda bi, i, c: (bi, jnp.minimum((i + 1) * hb, nhb - 1), 0)),
        ]
    in_specs = x_specs + [
        mspec,
        pl.BlockSpec(nw.shape, lambda bi, i, c: (0, 0)),
        pl.BlockSpec((d, ck), lambda bi, i, c: (0, c)),
        pl.BlockSpec((d, ck), lambda bi, i, c: (0, c + nc)),
        pl.BlockSpec((9, ck), lambda bi, i, c: (0, c)),
        pl.BlockSpec((9, ck), lambda bi, i, c: (0, c + nc)),
        pl.BlockSpec((1, ck), lambda bi, i, c: (0, c)),
        pl.BlockSpec((1, ck), lambda bi, i, c: (0, c + nc)),
        pl.BlockSpec((ck, d), lambda bi, i, c: (c, 0)),
        pl.BlockSpec(nf.shape, lambda bi, i, c: (0, 0)),
    ]
    rows = tt + 2 * width if vertical else tt
    kern = functools.partial(_ffn_kernel, width=width, vertical=vertical, final_norm=final_norm)
    out = pl.pallas_call(
        kern,
        grid=(b, nt, nc),
        in_specs=in_specs,
        out_specs=tok,
        out_shape=jax.ShapeDtypeStruct((b, t, d), F32),
        scratch_shapes=[pltpu.VMEM((rows, d), BF16), pltpu.VMEM((tt, d), F32)],
        compiler_params=_params("arbitrary", "arbitrary", "arbitrary"),
        name="conv_ffn",
    )(*xs, mod, nw, w_up, w_up, wc, wc, bc, bc, w_down, nf)
    return out.reshape(out_shape)


def _sgu_kernel(x_ref, mod_ref, nw_ref, win_ref, lnw_ref, lnb_ref, ws_ref, bs_ref, wout_ref, o_ref):
    d = x_ref.shape[2]
    tt = x_ref.shape[1]
    e = wout_ref.shape[0]
    groups = ws_ref.shape[0]
    gw = e // groups
    shift = mod_ref[0, :, 0:d]
    scale = mod_ref[0, :, d:2 * d]
    gate = mod_ref[0, :, 2 * d:3 * d]
    chunk_rows = [slice(p * SGU_CHUNK, (p + 1) * SGU_CHUNK) for p in range(tt // SGU_CHUNK)]
    xs = [x_ref[0, rows, :] for rows in chunk_rows]
    zs = [_dot(_norm_mod(x, nw_ref[...], shift, scale), win_ref[...]) for x in xs]
    for rows, x, z in zip(chunk_rows, xs, zs):
        z = 0.5 * z * (1.0 + lax.erf(z * (1.0 / math.sqrt(2.0))))
        u = z[:, :e]
        v = z[:, e:]
        mu = jnp.mean(v, axis=-1, keepdims=True)
        vc = v - mu
        var = jnp.mean(vc * vc, axis=-1, keepdims=True)
        vn = (vc * lax.rsqrt(var + LN_EPS) * lnw_ref[...] + lnb_ref[...]).astype(BF16)
        parts = [jnp.dot(ws_ref[gi], vn[:, gi * gw:(gi + 1) * gw], preferred_element_type=F32)
                 for gi in range(groups)]
        vm = jnp.concatenate(parts, axis=1) + bs_ref[...]
        out = _dot(u * vm, wout_ref[...])
        o_ref[0, rows, :] = x + gate * out


def _sgu(x, mod, nw, w_in, ln_w, ln_b, w_s, b_full, w_out):
    b, t, d = x.shape
    tt = TOKEN_TILE
    tok = pl.BlockSpec((1, tt, d), lambda bi, i: (bi, i, 0))
    consts = (nw, w_in, ln_w, ln_b, w_s, b_full, w_out)
    return pl.pallas_call(
        _sgu_kernel,
        grid=(b, t // tt),
        in_specs=[tok, _mod_spec(mod)] + [_const_spec(c) for c in consts],
        out_specs=tok,
        out_shape=jax.ShapeDtypeStruct((b, t, d), F32),
        compiler_params=_params("arbitrary", "arbitrary"),
        name="sgu",
    )(x, mod, *consts)


def _lane_param(p, heads):
    m = p.reshape(heads // 2, 2, HEAD_DIM).transpose(2, 1, 0)
    m = jnp.broadcast_to(m[:, :, None, None, :], (HEAD_DIM, 2, 2, SCAN_BATCH, heads // 2))
    return m.reshape(HEAD_DIM, LANES)


def _state_to_scan(sf, sb):
    b, h = sf.shape[:2]
    g = b // SCAN_BATCH
    s = jnp.stack([sf, sb], axis=0).reshape(2, g, SCAN_BATCH, h // 2, 2, HEAD_DIM, HEAD_DIM)
    return s.transpose(1, 6, 5, 4, 0, 2, 3).reshape(g, HEAD_DIM, HEAD_DIM, LANES)


def _state_from_scan(s, heads):
    g = s.shape[0]
    s = s.reshape(g, HEAD_DIM, HEAD_DIM, 2, 2, SCAN_BATCH, heads // 2)
    s = s.transpose(4, 0, 5, 6, 3, 2, 1).reshape(2, g * SCAN_BATCH, heads, HEAD_DIM, HEAD_DIM)
    return s[0], s[1]


def kernel(x_prompt, x_sample, state_ctx_fwd, state_ctx_bwd, c, c_ctx, ada_w, ada_b, norm_mix, norm_ffn, ffn_up, ffn_conv, ffn_conv_b, ffn_down, norm_final, rw_mu, rw_wr, rw_wk, rw_wv, rw_wo, rw_w0, rw_w1, rw_w2, rw_a0, rw_a1, rw_a2, rw_g1, rw_g2, rw_kk, rw_ka, rw_rk, rw_lnx_w, rw_lnx_b, sg_in, sg_ln_w, sg_ln_b, sg_ws, sg_bs, sg_out):
    depth, d, _ = ada_w.shape
    heads = d // HEAD_DIM
    n_dec = c.shape[0]
    assert SCAN_BATCH * heads * 2 == LANES
    assert n_dec + 1 <= SUBLANES

    cond8 = jnp.concatenate(
        [c_ctx[None, :], c, jnp.zeros((SUBLANES - 1 - n_dec, d), F32)], axis=0)
    mod_all = _modulation(cond8, ada_w, ada_b)
    nf = norm_final.reshape(1, d)

    def run_stream(x, mod_rows, s0f, s0b, width, want_state):
        b, t, _ = x.shape
        new_f, new_b = [], []
        for i in range(depth):
            mod = mod_all[i, mod_rows[0]:mod_rows[1]][:, None, :]
            j = i // 2
            nw_mix = norm_mix[i].reshape(1, d)
            if i % 2 == 0:
                w1 = jnp.concatenate([rw_w1[j, 0], rw_w1[j, 1]], axis=1).astype(BF16)
                a1 = jnp.concatenate([rw_a1[j, 0], rw_a1[j, 1]], axis=1).astype(BF16)
                r, k, v, df, db, af, ab, g = _rw_proj(
                    x, mod, nw_mix, rw_mu[j],
                    rw_wr[j].astype(BF16), rw_wk[j].astype(BF16), rw_wv[j].astype(BF16),
                    w1, a1, rw_g1[j].astype(BF16),
                    rw_w2[j].astype(BF16), rw_a2[j].astype(BF16), rw_w0[j], rw_a0[j])
                s0 = _state_to_scan(s0f[:, j], s0b[:, j])
                ys, bvs, sfin = _scan(
                    r, k, v, df, db, af, ab, s0,
                    _lane_param(rw_kk[j], heads), _lane_param(rw_ka[j], heads),
                    _lane_param(rw_rk[j].reshape(-1), heads))
                lo, hi = _rw_post(ys, bvs, _lane_param(rw_lnx_w[j], heads),
                                  _lane_param(rw_lnx_b[j], heads), d)
                yn = jnp.concatenate([lo, hi], axis=1)
                x = _rw_out(x, yn, g, mod, rw_g2[j].astype(BF16), rw_wo[j].astype(BF16))
                if want_state:
                    sf, sb = _state_from_scan(sfin, heads)
                    new_f.append(sf)
                    new_b.append(sb)
            else:
                e = sg_out.shape[1]
                b_full = jnp.repeat(sg_bs[j].T, e // sg_bs.shape[1], axis=1)
                x = _sgu(x, mod, nw_mix, sg_in[j].astype(BF16),
                         sg_ln_w[j].reshape(1, e), sg_ln_b[j].reshape(1, e),
                         sg_ws[j].astype(BF16), b_full, sg_out[j].astype(BF16))
            x = _ffn(x, mod, norm_ffn[i].reshape(1, d), ffn_up[i].astype(BF16), ffn_conv[i],
                     ffn_conv_b[i], ffn_down[i].astype(BF16), nf,
                     width=width, final_norm=(i == depth - 1))
        if want_state:
            return x, jnp.stack(new_f, axis=1), jnp.stack(new_b, axis=1)
        return x, None, None

    bp = x_prompt.shape[0]
    n_rwkv = state_ctx_fwd.shape[1]
    zero_state = jnp.zeros((bp, n_rwkv, heads, HEAD_DIM, HEAD_DIM), F32)
    y_prompt, new_f, new_b = run_stream(x_prompt, (0, 1), zero_state, zero_state,
                                        x_prompt.shape[1], True)
    y_sample, _, _ = run_stream(x_sample, (1, 1 + n_dec), state_ctx_fwd, state_ctx_bwd,
                                GRID_W, False)
    return (y_prompt, y_sample, new_f, new_b)
```

```python
import functools
import math

import jax
import jax.numpy as jnp
from jax import lax
from jax.experimental import pallas as pl
from jax.experimental.pallas import tpu as pltpu

RMS_EPS = 1e-6
LN_EPS = 1e-5
GN_EPS = 64e-5
HEAD_DIM = 64
SGU_CHUNK = 128
GRID_W = 64

LANES = 128
SUBLANES = 8
VMEM_LIMIT = 56 * 1024 * 1024

TOKEN_TILE = 256
RW_OUT_TILE = 512
SGU_TILE = 512
FFN_TILE = 512
FFN_CHANNEL_BLOCKS = 2
SCAN_BATCH = 4
SCAN_TCHUNK = 64
SCAN_PREP_AT = 52
SCAN_ACCS = 1
SCAN_SLOTS = 3
POST_TCHUNK = 64

BF16 = jnp.bfloat16
F32 = jnp.float32


def _params(*sem):
    return pltpu.CompilerParams(dimension_semantics=sem, vmem_limit_bytes=VMEM_LIMIT)


def _dot(a, b):
    return jnp.dot(a.astype(BF16), b, preferred_element_type=F32)


def _norm_mod(x, g, shift, scale):
    ms = jnp.mean(x * x, axis=-1, keepdims=True)
    return x * lax.rsqrt(ms + RMS_EPS) * g * (1.0 + scale) + shift


def _sigmoid(x):
    return 1.0 / (1.0 + jnp.exp(-x))


def _tree_sum(xs):
    while len(xs) > 1:
        xs = [xs[i] + xs[i + 1] for i in range(0, len(xs), 2)]
    return xs[0]


def _tile_major_shape(b, t, d):
    return (b, t // SUBLANES, d // LANES * SUBLANES, LANES)


def _put_tiles(ref, val):
    n = val.shape[0] // SUBLANES
    for p in range(val.shape[1] // LANES):
        ref[0, :, p * SUBLANES:(p + 1) * SUBLANES, :] = (
            val[:, p * LANES:(p + 1) * LANES].reshape(n, SUBLANES, LANES))


def _get_tiles(ref):
    _, n, rows, _ = ref.shape
    return jnp.concatenate(
        [ref[0, :, p * SUBLANES:(p + 1) * SUBLANES, :].reshape(n * SUBLANES, LANES)
         for p in range(rows // SUBLANES)], axis=1)


def _token_rows(u, pairs):
    return pl.ds(u, pairs, stride=SUBLANES)


def _mod_kernel(c_ref, w_ref, b_ref, o_ref):
    c = c_ref[...]
    sc = c * _sigmoid(c)
    o_ref[0] = jnp.dot(sc, w_ref[0], preferred_element_type=F32,
                       precision=lax.Precision.HIGHEST) + b_ref[0]


def _modulation(cond8, ada_w, ada_b):
    depth, d, n = ada_w.shape
    tn = n // 2
    return pl.pallas_call(
        _mod_kernel,
        grid=(depth, n // tn),
        in_specs=[
            pl.BlockSpec((SUBLANES, d), lambda l, j: (0, 0)),
            pl.BlockSpec((1, d, tn), lambda l, j: (l, 0, j)),
            pl.BlockSpec((1, 1, tn), lambda l, j: (l, 0, j)),
        ],
        out_specs=pl.BlockSpec((1, SUBLANES, tn), lambda l, j: (l, 0, j)),
        out_shape=jax.ShapeDtypeStruct((depth, SUBLANES, n), F32),
        compiler_params=_params("arbitrary", "arbitrary"),
        name="modulation",
    )(cond8, ada_w, ada_b.reshape(depth, 1, n))


def _mod_spec(mod):
    if mod.shape[0] == 1:
        return pl.BlockSpec((1, 1, mod.shape[2]), lambda b, i: (0, 0, 0))
    return pl.BlockSpec((1, 1, mod.shape[2]), lambda b, i: (b, 0, 0))


def _const_spec(a):
    nd = a.ndim
    return pl.BlockSpec(a.shape, lambda *_: (0,) * nd)


def _rw_proj_kernel(x_ref, xp_ref, xn_ref, mod_ref, nw_ref, mu_ref,
                    wr_ref, wk_ref, wv_ref, w1_ref, a1_ref, g1_ref,
                    w2_ref, a2_ref, w0_ref, a0_ref,
                    r_ref, k_ref, v_ref, df_ref, db_ref, af_ref, ab_ref, g_ref):
    i = pl.program_id(1)
    last = pl.num_programs(1) - 1
    d = x_ref.shape[2]
    tt = x_ref.shape[1]
    nw = nw_ref[...]
    shift = mod_ref[0, :, 0:d]
    scale = mod_ref[0, :, d:2 * d]
    h = _norm_mod(x_ref[0], nw, shift, scale)
    hp = _norm_mod(xp_ref[0], nw, shift, scale)[SUBLANES - 1:SUBLANES, :]
    hn = _norm_mod(xn_ref[0], nw, shift, scale)[0:1, :]
    hp = jnp.where(i == 0, 0.0, hp)
    hn = jnp.where(i == last, 0.0, hn)
    rows = lax.broadcasted_iota(jnp.int32, (tt, 1), 0)
    prev = jnp.where(rows == 0, hp, pltpu.roll(h, 1, axis=0))
    nxt = jnp.where(rows == tt - 1, hn, pltpu.roll(h, tt - 1, axis=0))
    dp = prev - h
    dn = nxt - h

    def shifted(idx):
        return h + dp * mu_ref[0, idx:idx + 1, :] + dn * mu_ref[1, idx:idx + 1, :]

    _put_tiles(r_ref, _dot(shifted(0), wr_ref[...]))
    _put_tiles(k_ref, _dot(shifted(2), wk_ref[...]))
    _put_tiles(v_ref, _dot(shifted(3), wv_ref[...]))
    g_ref[0] = _sigmoid(_dot(shifted(5), g1_ref[...]))
    lw = jnp.tanh(_dot(shifted(1), w1_ref[...]))
    la = _dot(shifted(4), a1_ref[...])
    lora = w2_ref.shape[1]
    decay_scale = math.exp(-0.5)
    for dirn, (d_ref, a_ref) in enumerate(((df_ref, af_ref), (db_ref, ab_ref))):
        w_raw = w0_ref[dirn:dirn + 1, :] + _dot(lw[:, dirn * lora:(dirn + 1) * lora], w2_ref[dirn])
        _put_tiles(d_ref, jnp.exp(-decay_scale * _sigmoid(w_raw)))
        a_raw = a0_ref[dirn:dirn + 1, :] + _dot(la[:, dirn * lora:(dirn + 1) * lora], a2_ref[dirn])
        _put_tiles(a_ref, _sigmoid(a_raw))


def _rw_proj(x, mod, nw, mu, wr, wk, wv, w1, a1, g1, w2, a2, w0, a0):
    b, t, d = x.shape
    tt = TOKEN_TILE
    nt = t // tt
    hb = tt // SUBLANES
    nhb = t // SUBLANES
    tok = pl.BlockSpec((1, tt, d), lambda bi, i: (bi, i, 0))
    prev = pl.BlockSpec((1, SUBLANES, d), lambda bi, i: (bi, jnp.maximum(i * hb - 1, 0), 0))
    nxt = pl.BlockSpec((1, SUBLANES, d), lambda bi, i: (bi, jnp.minimum((i + 1) * hb, nhb - 1), 0))
    consts = (nw, mu, wr, wk, wv, w1, a1, g1, w2, a2, w0, a0)
    gl = g1.shape[1]
    tiles = _tile_major_shape(b, t, d)
    tile_spec = pl.BlockSpec((1, tt // SUBLANES) + tiles[2:], lambda bi, i: (bi, i, 0, 0))
    out_shape = [jax.ShapeDtypeStruct(tiles, F32)] * 7 + [jax.ShapeDtypeStruct((b, t, gl), F32)]
    out_specs = [tile_spec] * 7 + [pl.BlockSpec((1, tt, gl), lambda bi, i: (bi, i, 0))]
    return pl.pallas_call(
        _rw_proj_kernel,
        grid=(b, nt),
        in_specs=[tok, prev, nxt, _mod_spec(mod)] + [_const_spec(c) for c in consts],
        out_specs=out_specs,
        out_shape=out_shape,
        compiler_params=_params("arbitrary", "arbitrary"),
        name="rw_proj",
    )(x, x, x, mod, *consts)


def _scan_kernel(rf_ref, rm_ref, kf_ref, km_ref, vf_ref, vm_ref, wf_ref, wm_ref, af_ref, am_ref,
                 s0_ref, kk_ref, ka_ref, rk_ref, y_ref, bv_ref, sfin_ref,
                 s_ref, ops_ref, sa_ref, stage_ref, cols_ref):
    c = pl.program_id(1)
    nb, ntb, pairs = rf_ref.shape[0], rf_ref.shape[1], rf_ref.shape[2] // SUBLANES
    tc = ntb * SUBLANES
    hd = s_ref.shape[0]

    @pl.when(c == 0)
    def _():
        s_ref[...] = s0_ref[0]

    operands = ((rf_ref, rm_ref), (kf_ref, km_ref), (vf_ref, vm_ref), (af_ref, am_ref),
                (wf_ref, wm_ref))

    def gather_rows(tb):
        half = tb % 2
        for op, (fwd_ref, bwd_ref) in enumerate(operands):
            for b in range(nb):
                tiles = fwd_ref[b, tb].reshape(pairs, SUBLANES, LANES)
                stage_ref[half, op, :, b * pairs:(b + 1) * pairs, :] = jnp.swapaxes(tiles, 0, 1)
                tiles = bwd_ref[b, ntb - 1 - tb].reshape(pairs, SUBLANES, LANES)
                by_token = jnp.swapaxes(tiles, 0, 1)
                for u in range(SUBLANES):
                    stage_ref[half, op, SUBLANES - 1 - u, (nb + b) * pairs:(nb + b + 1) * pairs, :] = (
                        by_token[u])

    def transpose_rows(s):
        half = (s // SUBLANES) % 2
        u = s % SUBLANES
        for op in range(len(operands)):
            cols = stage_ref[half, op, u].T
            cols_ref[op] = jnp.concatenate([cols[0:hd], cols[hd:2 * hd]], axis=1)

    def prepare(s, slot):
        r = cols_ref[0]
        kraw = cols_ref[1]
        v = cols_ref[2]
        a = cols_ref[3]
        kkv = kraw * kk_ref[...]
        nrm = jnp.sqrt(jnp.sum(kkv * kkv, axis=0, keepdims=True))
        kk = kkv / jnp.maximum(nrm, 1e-12)
        kd = kraw * (1.0 + (a - 1.0) * ka_ref[...])
        p_prev = ops_ref[(s + SCAN_SLOTS - 1) % SCAN_SLOTS, 5]
        p = p_prev * cols_ref[4]
        inv_p = 1.0 / p
        ops_ref[slot, 0] = -kk * p_prev
        ops_ref[slot, 1] = kk * a * inv_p
        ops_ref[slot, 2] = kd * inv_p
        ops_ref[slot, 3] = r * p
        ops_ref[slot, 4] = v
        ops_ref[slot, 5] = p
        bv_ref[0, s] = v * jnp.sum(r * kd * rk_ref[...], axis=0, keepdims=True)

    def accumulate(acc, j, term):
        n = j % SCAN_ACCS
        acc[n] = term if acc[n] is None else acc[n] + term

    def first_sa(slot):
        acc = [None] * SCAN_ACCS
        for j in range(hd):
            accumulate(acc, j, s_ref[j] * ops_ref[slot, 0, j:j + 1, :])
        sa_ref[...] = _tree_sum(acc)

    def advance(s, slot, nxt, midway):
        sa = sa_ref[...]
        vr = ops_ref[slot, 4]
        acc_y = [None] * SCAN_ACCS
        acc_a = [None] * SCAN_ACCS
        for j in range(hd):
            if j == SCAN_PREP_AT:
                midway()
            sn = s_ref[j] + sa * ops_ref[slot, 1, j:j + 1, :] + vr * ops_ref[slot, 2, j:j + 1, :]
            s_ref[j] = sn
            accumulate(acc_y, j, sn * ops_ref[slot, 3, j:j + 1, :])
            accumulate(acc_a, j, sn * ops_ref[nxt, 0, j:j + 1, :])
        y_ref[0, s] = _tree_sum(acc_y)
        sa_ref[...] = _tree_sum(acc_a)

    ops_ref[SCAN_SLOTS - 1, 5] = jnp.ones((hd, LANES), F32)
    gather_rows(0)
    for s in range(2):
        transpose_rows(s)
        prepare(s, s)
    first_sa(0)

    def step(s, carry):
        ahead = jnp.minimum(s + 2, tc - 1)
        transpose_rows(ahead)
        advance(s, s % SCAN_SLOTS, (s + 1) % SCAN_SLOTS,
                functools.partial(prepare, ahead, (s + 2) % SCAN_SLOTS))
        return carry

    def token_block(tb, carry):
        gather_rows(jnp.minimum(tb + 1, ntb - 1))
        return lax.fori_loop(tb * SUBLANES, (tb + 1) * SUBLANES, step, carry)

    lax.fori_loop(0, ntb, token_block, 0)

    for j in range(hd):
        s_ref[j] = s_ref[j] * ops_ref[(tc - 1) % SCAN_SLOTS, 5, j:j + 1, :]

    @pl.when(c == pl.num_programs(1) - 1)
    def _():
        sfin_ref[0] = s_ref[...]


def _scan(r, k, v, wf, wb, af, ab, s0, kk, ka, rk):
    b, tblocks, _, _ = r.shape
    t = tblocks * SUBLANES
    hd = HEAD_DIM
    g = b // SCAN_BATCH
    tc = SCAN_TCHUNK
    nt = t // tc
    blk = (SCAN_BATCH, tc // SUBLANES) + r.shape[2:]
    fwd = pl.BlockSpec(blk, lambda gi, c: (gi, c, 0, 0))
    bwd = pl.BlockSpec(blk, lambda gi, c: (gi, nt - 1 - c, 0, 0))
    step = pl.BlockSpec((1, tc, hd, LANES), lambda gi, c: (gi, c, 0, 0))
    state = pl.BlockSpec((1, hd, hd, LANES), lambda gi, c: (gi, 0, 0, 0))
    steps = jax.ShapeDtypeStruct((g, t, hd, LANES), F32)
    return pl.pallas_call(
        _scan_kernel,
        grid=(g, nt),
        in_specs=[fwd, bwd] * 5 + [state, _const_spec(kk), _const_spec(ka), _const_spec(rk)],
        out_specs=[step, step, state],
        out_shape=[steps, steps, jax.ShapeDtypeStruct((g, hd, hd, LANES), F32)],
        scratch_shapes=[pltpu.VMEM((hd, hd, LANES), F32),
                        pltpu.VMEM((SCAN_SLOTS, 6, hd, LANES), F32),
                        pltpu.VMEM((hd, LANES), F32),
                        pltpu.VMEM((2, 5, SUBLANES, 2 * SCAN_BATCH * r.shape[2] // SUBLANES, LANES), F32),
                        pltpu.VMEM((5, hd, LANES), F32)],
        compiler_params=_params("arbitrary", "arbitrary"),
        name="wkv_scan",
    )(r, r, k, k, v, v, wf, wb, af, ab, s0, kk, ka, rk)


def _rw_post_kernel(y_ref, ym_ref, bv_ref, bvm_ref, lw_ref, lb_ref, lo_ref, hi_ref):
    tp = y_ref.shape[1]
    hd = y_ref.shape[2]
    nb, ntb, pairs = lo_ref.shape[0], lo_ref.shape[1], lo_ref.shape[2] // SUBLANES
    quarter = LANES // 4
    lane = lax.broadcasted_iota(jnp.int32, (hd, LANES), 1)
    fwd_lane = (lane % (2 * quarter)) < quarter

    def swap_dirs(x):
        return jnp.where(fwd_lane, pltpu.roll(x, LANES - quarter, axis=1), pltpu.roll(x, quarter, axis=1))

    def token_block(tb, carry):
        ts = [tb * SUBLANES + u for u in range(SUBLANES)]
        ys = [y_ref[0, t] + swap_dirs(ym_ref[0, tp - 1 - t]) for t in ts]
        bonuses = [bv_ref[0, t] + swap_dirs(bvm_ref[0, tp - 1 - t]) for t in ts]
        outs = []
        for y, bonus in zip(ys, bonuses):
            mean = jnp.mean(y, axis=0, keepdims=True)
            yc = y - mean
            var = jnp.mean(yc * yc, axis=0, keepdims=True)
            yn = yc * lax.rsqrt(var + GN_EPS) * lw_ref[...] + lb_ref[...]
            outs.append(yn + bonus)
        cols = [o.T for o in outs]
        tiles = [jnp.concatenate([c[0:hd], c[hd:2 * hd]], axis=1) for c in cols]
        for u, rows in enumerate(tiles):
            for b in range(nb):
                lo_ref[b, tb, _token_rows(u, pairs), :] = rows[b * pairs:(b + 1) * pairs]
                hi_ref[b, ntb - 1 - tb, _token_rows(SUBLANES - 1 - u, pairs), :] = (
                    rows[(nb + b) * pairs:(nb + b + 1) * pairs])
        return carry

    lax.fori_loop(0, ntb, token_block, 0)


def _rw_post(y, bv, lw, lb, d):
    g, t, hd, _ = y.shape
    tp = POST_TCHUNK
    nb = t // tp
    cur = pl.BlockSpec((1, tp, hd, LANES), lambda gi, c: (gi, c, 0, 0))
    mir = pl.BlockSpec((1, tp, hd, LANES), lambda gi, c: (gi, nb - 1 - c, 0, 0))
    tiles = _tile_major_shape(g * SCAN_BATCH, t // 2, d)
    blk = (SCAN_BATCH, tp // SUBLANES) + tiles[2:]
    lo = pl.BlockSpec(blk, lambda gi, c: (gi, c, 0, 0))
    hi = pl.BlockSpec(blk, lambda gi, c: (gi, nb // 2 - 1 - c, 0, 0))
    half_tokens = jax.ShapeDtypeStruct(tiles, F32)
    return pl.pallas_call(
        _rw_post_kernel,
        grid=(g, nb // 2),
        in_specs=[cur, mir, cur, mir, _const_spec(lw), _const_spec(lb)],
        out_specs=[lo, hi],
        out_shape=[half_tokens, half_tokens],
        compiler_params=_params("arbitrary", "arbitrary"),
        name="rw_post",
    )(y, y, bv, bv, lw, lb)


def _rw_out_kernel(x_ref, yn_ref, g_ref, mod_ref, g2_ref, wo_ref, o_ref):
    d = x_ref.shape[2]
    gate = mod_ref[0, :, 2 * d:3 * d]
    g = _dot(g_ref[0], g2_ref[...])
    out = _dot(_get_tiles(yn_ref) * g, wo_ref[...])
    o_ref[0] = x_ref[0] + gate * out


def _rw_out(x, yn, g, mod, g2, wo):
    b, t, d = x.shape
    tt = min(RW_OUT_TILE, t)
    tok = pl.BlockSpec((1, tt, d), lambda bi, i: (bi, i, 0))
    tiles = pl.BlockSpec((1, tt // SUBLANES) + yn.shape[2:], lambda bi, i: (bi, i, 0, 0))
    gspec = pl.BlockSpec((1, tt, g.shape[2]), lambda bi, i: (bi, i, 0))
    return pl.pallas_call(
        _rw_out_kernel,
        grid=(b, t // tt),
        in_specs=[tok, tiles, gspec, _mod_spec(mod), _const_spec(g2), _const_spec(wo)],
        out_specs=tok,
        out_shape=jax.ShapeDtypeStruct((b, t, d), F32),
        compiler_params=_params("arbitrary", "arbitrary"),
        name="rw_out",
    )(x, yn, g, mod, g2, wo)


def _conv_taps(cur, up, down, w_ref, width, vertical):
    n = cur.shape[0]
    col = lax.broadcasted_iota(jnp.int32, (n, 1), 0) % width
    if vertical:
        ext = jnp.concatenate([up, cur, down], axis=0)
        slabs = [(0, ext[0:n]), (1, cur), (2, ext[2 * width:2 * width + n])]
    else:
        slabs = [(1, cur)]
    z = []
    for dc in range(3):
        acc = None
        for dr, rows in slabs:
            term = rows * w_ref[3 * dr + dc:3 * dr + dc + 1, :]
            acc = term if acc is None else acc + term
        z.append(acc)
    left = jnp.where(col != 0, pltpu.roll(z[0], 1, axis=0), 0.0)
    right = jnp.where(col != width - 1, pltpu.roll(z[2], n - 1, axis=0), 0.0)
    return left + z[1] + right


def _ffn_kernel(*refs, width, vertical, final_norm):
    if vertical:
        (x_ref, xup_ref, xdn_ref, mod_ref, nw_ref, wuv_ref, wug_ref,
         wcv_ref, wcg_ref, bv_ref, bg_ref, wd_ref, nf_ref, o_ref, h_ref, acc_ref) = refs
    else:
        (x_ref, mod_ref, nw_ref, wuv_ref, wug_ref,
         wcv_ref, wcg_ref, bv_ref, bg_ref, wd_ref, nf_ref, o_ref, h_ref, acc_ref) = refs
    i = pl.program_id(1)
    c = pl.program_id(2)
    tt, d = x_ref.shape[1], x_ref.shape[2]
    pad = width if vertical else 0

    @pl.when(c == 0)
    def _():
        shift = mod_ref[0, :, 3 * d:4 * d]
        scale = mod_ref[0, :, 4 * d:5 * d]
        h_ref[pad:pad + tt, :] = _norm_mod(x_ref[0], nw_ref[...], shift, scale).astype(BF16)
        if vertical:
            h_ref[0:pad, :] = _norm_mod(xup_ref[0], nw_ref[...], shift, scale).astype(BF16)
            h_ref[pad + tt:, :] = _norm_mod(xdn_ref[0], nw_ref[...], shift, scale).astype(BF16)

    top = i == 0
    bottom = i == pl.num_programs(1) - 1

    def conv(w_up_ref, w_conv_ref):
        up = jnp.dot(h_ref[...], w_up_ref[...], preferred_element_type=F32)
        if not vertical:
            return _conv_taps(up, None, None, w_conv_ref, width, vertical)
        above = jnp.where(top, 0.0, up[0:pad])
        below = jnp.where(bottom, 0.0, up[pad + tt:])
        return _conv_taps(up[pad:pad + tt], above, below, w_conv_ref, width, vertical)

    val = conv(wuv_ref, wcv_ref) + bv_ref[...]
    gat = conv(wug_ref, wcg_ref) + bg_ref[...]
    act = gat * _sigmoid(gat) * val
    part = _dot(act, wd_ref[...])

    @pl.when(c == 0)
    def _():
        acc_ref[...] = part

    @pl.when(c != 0)
    def _():
        acc_ref[...] += part

    @pl.when(c == pl.num_programs(2) - 1)
    def _():
        gate = mod_ref[0, :, 5 * d:6 * d]
        y = x_ref[0] + gate * acc_ref[...]
        if final_norm:
            ms = jnp.mean(y * y, axis=-1, keepdims=True)
            y = y * lax.rsqrt(ms + RMS_EPS) * nf_ref[...]
        o_ref[0] = y


def _ffn(x, mod, nw, w_up, w_conv, b_conv, w_down, nf, *, width, final_norm):
    out_shape = x.shape
    vertical = width < x.shape[1]
    merge = FFN_TILE // x.shape[1]
    if not vertical and mod.shape[0] == 1 and merge > 1 and x.shape[0] % merge == 0:
        x = x.reshape(x.shape[0] // merge, merge * x.shape[1], x.shape[2])
    b, t, d = x.shape
    f = w_down.shape[0]
    tt = min(FFN_TILE, t)
    nt = t // tt
    ck = f // FFN_CHANNEL_BLOCKS
    nc = FFN_CHANNEL_BLOCKS
    wc = w_conv.reshape(9, 2 * f)
    bc = b_conv.reshape(1, 2 * f)
    hb = tt // width if vertical else 1
    nhb = t // width if vertical else 1

    if mod.shape[0] == 1:
        mspec = pl.BlockSpec((1, 1, mod.shape[2]), lambda bi, i, c: (0, 0, 0))
    else:
        mspec = pl.BlockSpec((1, 1, mod.shape[2]), lambda bi, i, c: (bi, 0, 0))
    tok = pl.BlockSpec((1, tt, d), lambda bi, i, c: (bi, i, 0))
    xs = [x]
    x_specs = [tok]
    if vertical:
        xs += [x, x]
        x_specs += [
            pl.BlockSpec((1, width, d), lambda bi, i, c: (bi, jnp.maximum(i * hb - 1, 0), 0)),
            pl.BlockSpec((1, width, d), lambda bi, i, c: (bi, jnp.minimum((i + 1) * hb, nhb - 1), 0)),
        ]
    in_specs = x_specs + [
        mspec,
        pl.BlockSpec(nw.shape, lambda bi, i, c: (0, 0)),
        pl.BlockSpec((d, ck), lambda bi, i, c: (0, c)),
        pl.BlockSpec((d, ck), lambda bi, i, c: (0, c + nc)),
        pl.BlockSpec((9, ck), lambda bi, i, c: (0, c)),
        pl.BlockSpec((9, ck), lambda bi, i, c: (0, c + nc)),
        pl.BlockSpec((1, ck), lambda bi, i, c: (0, c)),
        pl.BlockSpec((1, ck), lambda bi, i, c: (0, c + nc)),
        pl.BlockSpec((ck, d), lambda bi, i, c: (c, 0)),
        pl.BlockSpec(nf.shape, lambda bi, i, c: (0, 0)),
    ]
    rows = tt + 2 * width if vertical else tt
    kern = functools.partial(_ffn_kernel, width=width, vertical=vertical, final_norm=final_norm)
    out = pl.pallas_call(
        kern,
        grid=(b, nt, nc),
        in_specs=in_specs,
        out_specs=tok,
        out_shape=jax.ShapeDtypeStruct((b, t, d), F32),
        scratch_shapes=[pltpu.VMEM((rows, d), BF16), pltpu.VMEM((tt, d), F32)],
        compiler_params=_params("arbitrary", "arbitrary", "arbitrary"),
        name="conv_ffn",
    )(*xs, mod, nw, w_up, w_up, wc, wc, bc, bc, w_down, nf)
    return out.reshape(out_shape)


def _sgu_kernel(x_ref, mod_ref, nw_ref, win_ref, lnw_ref, lnb_ref, ws_ref, bs_ref, wout_ref, o_ref):
    d = x_ref.shape[2]
    tt = x_ref.shape[1]
    e = wout_ref.shape[0]
    groups = ws_ref.shape[0]
    gw = e // groups
    shift = mod_ref[0, :, 0:d]
    scale = mod_ref[0, :, d:2 * d]
    gate = mod_ref[0, :, 2 * d:3 * d]
    chunk_rows = [slice(p * SGU_CHUNK, (p + 1) * SGU_CHUNK) for p in range(tt // SGU_CHUNK)]
    xs = [x_ref[0, rows, :] for rows in chunk_rows]
    zs = [_dot(_norm_mod(x, nw_ref[...], shift, scale), win_ref[...]) for x in xs]
    for rows, x, z in zip(chunk_rows, xs, zs):
        z = 0.5 * z * (1.0 + lax.erf(z * (1.0 / math.sqrt(2.0))))
        u = z[:, :e]
        v = z[:, e:]
        mu = jnp.mean(v, axis=-1, keepdims=True)
        vc = v - mu
        var = jnp.mean(vc * vc, axis=-1, keepdims=True)
        vn = (vc * lax.rsqrt(var + LN_EPS) * lnw_ref[...] + lnb_ref[...]).astype(BF16)
        parts = [jnp.dot(ws_ref[gi], vn[:, gi * gw:(gi + 1) * gw], preferred_element_type=F32)
                 for gi in range(groups)]
        vm = jnp.concatenate(parts, axis=1) + bs_ref[...]
        out = _dot(u * vm, wout_ref[...])
        o_ref[0, rows, :] = x + gate * out


def _sgu(x, mod, nw, w_in, ln_w, ln_b, w_s, b_full, w_out):
    b, t, d = x.shape
    tt = min(SGU_TILE, t)
    tok = pl.BlockSpec((1, tt, d), lambda bi, i: (bi, i, 0))
    consts = (nw, w_in, ln_w, ln_b, w_s, b_full, w_out)
    return pl.pallas_call(
        _sgu_kernel,
        grid=(b, t // tt),
        in_specs=[tok, _mod_spec(mod)] + [_const_spec(c) for c in consts],
        out_specs=tok,
        out_shape=jax.ShapeDtypeStruct((b, t, d), F32),
        compiler_params=_params("arbitrary", "arbitrary"),
        name="sgu",
    )(x, mod, *consts)


def _lane_param(p, heads):
    m = p.reshape(heads // 2, 2, HEAD_DIM).transpose(2, 1, 0)
    m = jnp.broadcast_to(m[:, :, None, None, :], (HEAD_DIM, 2, 2, SCAN_BATCH, heads // 2))
    return m.reshape(HEAD_DIM, LANES)


def _state_to_scan(sf, sb):
    b, h = sf.shape[:2]
    g = b // SCAN_BATCH
    s = jnp.stack([sf, sb], axis=0).reshape(2, g, SCAN_BATCH, h // 2, 2, HEAD_DIM, HEAD_DIM)
    return s.transpose(1, 6, 5, 4, 0, 2, 3).reshape(g, HEAD_DIM, HEAD_DIM, LANES)


def _state_from_scan(s, heads):
    g = s.shape[0]
    s = s.reshape(g, HEAD_DIM, HEAD_DIM, 2, 2, SCAN_BATCH, heads // 2)
    s = s.transpose(4, 0, 5, 6, 3, 2, 1).reshape(2, g * SCAN_BATCH, heads, HEAD_DIM, HEAD_DIM)
    return s[0], s[1]


def kernel(x_prompt, x_sample, state_ctx_fwd, state_ctx_bwd, c, c_ctx, ada_w, ada_b, norm_mix, norm_ffn, ffn_up, ffn_conv, ffn_conv_b, ffn_down, norm_final, rw_mu, rw_wr, rw_wk, rw_wv, rw_wo, rw_w0, rw_w1, rw_w2, rw_a0, rw_a1, rw_a2, rw_g1, rw_g2, rw_kk, rw_ka, rw_rk, rw_lnx_w, rw_lnx_b, sg_in, sg_ln_w, sg_ln_b, sg_ws, sg_bs, sg_out):
    depth, d, _ = ada_w.shape
    heads = d // HEAD_DIM
    n_dec = c.shape[0]
    assert SCAN_BATCH * heads * 2 == LANES
    assert n_dec + 1 <= SUBLANES

    cond8 = jnp.concatenate(
        [c_ctx[None, :], c, jnp.zeros((SUBLANES - 1 - n_dec, d), F32)], axis=0)
    mod_all = _modulation(cond8, ada_w, ada_b)
    nf = norm_final.reshape(1, d)

    def run_stream(x, mod_rows, s0f, s0b, width, want_state):
        b, t, _ = x.shape
        new_f, new_b = [], []
        for i in range(depth):
            mod = mod_all[i, mod_rows[0]:mod_rows[1]][:, None, :]
            j = i // 2
            nw_mix = norm_mix[i].reshape(1, d)
            if i % 2 == 0:
                w1 = jnp.concatenate([rw_w1[j, 0], rw_w1[j, 1]], axis=1).astype(BF16)
                a1 = jnp.concatenate([rw_a1[j, 0], rw_a1[j, 1]], axis=1).astype(BF16)
                r, k, v, df, db, af, ab, g = _rw_proj(
                    x, mod, nw_mix, rw_mu[j],
                    rw_wr[j].astype(BF16), rw_wk[j].astype(BF16), rw_wv[j].astype(BF16),
                    w1, a1, rw_g1[j].astype(BF16),
                    rw_w2[j].astype(BF16), rw_a2[j].astype(BF16), rw_w0[j], rw_a0[j])
                s0 = _state_to_scan(s0f[:, j], s0b[:, j])
                ys, bvs, sfin = _scan(
                    r, k, v, df, db, af, ab, s0,
                    _lane_param(rw_kk[j], heads), _lane_param(rw_ka[j], heads),
                    _lane_param(rw_rk[j].reshape(-1), heads))
                lo, hi = _rw_post(ys, bvs, _lane_param(rw_lnx_w[j], heads),
                                  _lane_param(rw_lnx_b[j], heads), d)
                yn = jnp.concatenate([lo, hi], axis=1)
                x = _rw_out(x, yn, g, mod, rw_g2[j].astype(BF16), rw_wo[j].astype(BF16))
                if want_state:
                    sf, sb = _state_from_scan(sfin, heads)
                    new_f.append(sf)
                    new_b.append(sb)
            else:
                e = sg_out.shape[1]
                b_full = jnp.repeat(sg_bs[j].T, e // sg_bs.shape[1], axis=1)
                x = _sgu(x, mod, nw_mix, sg_in[j].astype(BF16),
                         sg_ln_w[j].reshape(1, e), sg_ln_b[j].reshape(1, e),
                         sg_ws[j].astype(BF16), b_full, sg_out[j].astype(BF16))
            x = _ffn(x, mod, norm_ffn[i].reshape(1, d), ffn_up[i].astype(BF16), ffn_conv[i],
                     ffn_conv_b[i], ffn_down[i].astype(BF16), nf,
                     width=width, final_norm=(i == depth - 1))
        if want_state:
            return x, jnp.stack(new_f, axis=1), jnp.stack(new_b, axis=1)
        return x, None, None

    bp = x_prompt.shape[0]
    n_rwkv = state_ctx_fwd.shape[1]
    zero_state = jnp.zeros((bp, n_rwkv, heads, HEAD_DIM, HEAD_DIM), F32)
    y_prompt, new_f, new_b = run_stream(x_prompt, (0, 1), zero_state, zero_state,
                                        x_prompt.shape[1], True)
    y_sample, _, _ = run_stream(x_sample, (1, 1 + n_dec), state_ctx_fwd, state_ctx_bwd,
                                GRID_W, False)
    return (y_prompt, y_sample, new_f, new_b)
```
